```python
import math
import jax, jax.numpy as jnp
from jax import lax
import numpy as np

D_MODEL = 4096
BATCH = 1
SEQ = 16384
DEPTH = 2

N_MIXERS = 2
ATTN_HEADS = 16
ATTN_HEAD_DIM = D_MODEL // (2 * ATTN_HEADS)
ATTN_V_DIM = 2 * ATTN_HEAD_DIM
ATTN_WIDTH = ATTN_HEADS * ATTN_V_DIM
Q_BLOCK = 128
NUM_BUCKETS = 32
MAX_DISTANCE = 128
POOL_EXPAND = 2
POOL_WIDTH = POOL_EXPAND * D_MODEL
POOL_WINDOWS = (2, 4, 8, 16)
POOL_GROUPS = len(POOL_WINDOWS)
POOL_GROUP_DIM = POOL_WIDTH // POOL_GROUPS
N_ATTN_LAYERS = (DEPTH + 1) // 2
N_POOL_LAYERS = DEPTH // 2
NORM_EPS = 1e-6
SUBLN_EPS = 1e-5
NEG_INF = -1e30

kernel_name = 'hybrid_diffattn_multiscale_pool'


def rmsnorm(x, gain, eps):
    xf = x.astype(jnp.float32)
    inv = lax.rsqrt(jnp.mean(xf * xf, axis=-1, keepdims=True) + eps)
    return (xf * inv * gain.astype(jnp.float32)).astype(x.dtype)


def t5_causal_bucket(n):
    max_exact = NUM_BUCKETS // 2
    nf = jnp.maximum(n, max_exact).astype(jnp.float32)
    large = max_exact + (jnp.log(nf / max_exact) / math.log(MAX_DISTANCE / max_exact)
                         * (NUM_BUCKETS - max_exact)).astype(jnp.int32)
    large = jnp.minimum(large, NUM_BUCKETS - 1)
    return jnp.where(n < max_exact, n, large)


def lambda_init_fn(layer_idx):
    return 0.8 - 0.6 * math.exp(-0.3 * layer_idx)


def diff_attention_branch(h, w_in, lam_params, subln_gain, w_out, rel_bias, layer_idx):
    B, S, _ = h.shape
    n_blocks = S // Q_BLOCK
    proj = h @ w_in
    q, k, v, z = jnp.split(proj, 4, axis=-1)
    q = q.reshape(B, S, ATTN_HEADS, 2, ATTN_HEAD_DIM)
    k = k.reshape(B, S, ATTN_HEADS, 2, ATTN_HEAD_DIM)
    k1 = jnp.transpose(k[..., 0, :], (0, 2, 1, 3))
    k2 = jnp.transpose(k[..., 1, :], (0, 2, 1, 3))
    v = jnp.transpose(v.reshape(B, S, ATTN_HEADS, ATTN_V_DIM), (0, 2, 1, 3))

    def to_blocks(t):
        return jnp.transpose(t.reshape(B, n_blocks, Q_BLOCK, ATTN_HEADS, ATTN_HEAD_DIM), (1, 0, 3, 2, 4))

    q1b = to_blocks(q[..., 0, :])
    q2b = to_blocks(q[..., 1, :])

    lp = lam_params.astype(jnp.float32)
    lam_init = lambda_init_fn(layer_idx)
    lam = jnp.exp(jnp.sum(lp[0] * lp[1])) - jnp.exp(jnp.sum(lp[2] * lp[3])) + lam_init
    scale = ATTN_HEAD_DIM ** -0.5
    k_pos = jnp.arange(S)

    def block_fn(args):
        blk, qa, qb = args
        q_pos = blk * Q_BLOCK + jnp.arange(Q_BLOCK)
        dist = q_pos[:, None] - k_pos[None, :]
        causal = dist >= 0
        bias = jnp.transpose(rel_bias[t5_causal_bucket(jnp.maximum(dist, 0))].astype(jnp.float32), (2, 0, 1))

        def probs(qq, kk):
            logits = jnp.einsum('bhqd,bhkd->bhqk', qq, kk, preferred_element_type=jnp.float32) * scale + bias
            logits = jnp.where(causal, logits, NEG_INF)
            return jax.nn.softmax(logits, axis=-1)

        a = probs(qa, k1) - lam * probs(qb, k2)
        return jnp.einsum('bhqk,bhkd->bhqd', a.astype(v.dtype), v)

    o = lax.map(block_fn, (jnp.arange(n_blocks), q1b, q2b))
    o = jnp.transpose(o, (1, 0, 3, 2, 4)).reshape(B, S, ATTN_HEADS, ATTN_V_DIM)
    o = rmsnorm(o, subln_gain, SUBLN_EPS) * (1.0 - lam_init)
    o = o.reshape(B, S, ATTN_WIDTH) * jax.nn.silu(z)
    return o @ w_out


def pool_branch(h, w_in, w_group, scale, w_out):
    B, S, _ = h.shape
    u, z = jnp.split(h @ w_in, 2, axis=-1)
    ug = u.reshape(B, S, POOL_GROUPS, POOL_GROUP_DIM).astype(jnp.float32)
    cs = lax.cumsum(ug, axis=1)
    t = jnp.arange(S)
    pooled = []
    for g, w in enumerate(POOL_WINDOWS):
        c = cs[:, :, g]
        lag = jnp.pad(c, ((0, 0), (w, 0), (0, 0)))[:, :S]
        cnt = jnp.minimum(t + 1, w).astype(jnp.float32)[None, :, None]
        pooled.append((c - lag) / cnt - ug[:, :, g])
    p = jnp.stack(pooled, axis=2).astype(h.dtype)
    mixed = jnp.einsum('bsgc,gcd->bsgd', p, w_group).reshape(B, S, POOL_WIDTH) * scale
    return (mixed * jax.nn.silu(z)) @ w_out


def setup_inputs(seed: int = 0) -> dict:
    key = jax.random.key(seed)
    ks = jax.random.split(key, 13)
    f32 = jnp.float32
    x = jax.random.normal(ks[0], (BATCH, SEQ, D_MODEL), f32)
    norm_gains = 1.0 + 0.02 * jax.random.normal(ks[1], (DEPTH, D_MODEL), f32)
    final_norm_gain = 1.0 + 0.02 * jax.random.normal(ks[2], (D_MODEL,), f32)
    rel_bias = 0.2 * jax.random.normal(ks[3], (NUM_BUCKETS, ATTN_HEADS), f32)
    attn_w_in = jax.random.normal(ks[4], (N_ATTN_LAYERS, D_MODEL, 4 * ATTN_WIDTH), f32) * D_MODEL ** -0.5
    attn_lambda = 0.1 * jax.random.normal(ks[5], (N_ATTN_LAYERS, 4, ATTN_HEAD_DIM), f32)
    attn_subln_gain = 1.0 + 0.02 * jax.random.normal(ks[6], (N_ATTN_LAYERS, ATTN_V_DIM), f32)
    attn_w_out = jax.random.normal(ks[7], (N_ATTN_LAYERS, ATTN_WIDTH, D_MODEL), f32) * ATTN_WIDTH ** -0.5
    pool_w_in = jax.random.normal(ks[8], (N_POOL_LAYERS, D_MODEL, 2 * POOL_WIDTH), f32) * D_MODEL ** -0.5
    pool_w_group = jax.random.normal(ks[9], (N_POOL_LAYERS, POOL_GROUPS, POOL_GROUP_DIM, POOL_GROUP_DIM), f32) * POOL_GROUP_DIM ** -0.5
    pool_scale = 1.0 + 0.02 * jax.random.normal(ks[10], (N_POOL_LAYERS, POOL_WIDTH), f32)
    pool_w_out = jax.random.normal(ks[11], (N_POOL_LAYERS, POOL_WIDTH, D_MODEL), f32) * POOL_WIDTH ** -0.5
    return {'x': x, 'norm_gains': norm_gains, 'final_norm_gain': final_norm_gain, 'rel_bias': rel_bias,
            'attn_w_in': attn_w_in, 'attn_lambda': attn_lambda, 'attn_subln_gain': attn_subln_gain,
            'attn_w_out': attn_w_out, 'pool_w_in': pool_w_in, 'pool_w_group': pool_w_group,
            'pool_scale': pool_scale, 'pool_w_out': pool_w_out}


def reference(x, norm_gains, final_norm_gain, rel_bias, attn_w_in, attn_lambda, attn_subln_gain,
              attn_w_out, pool_w_in, pool_w_group, pool_scale, pool_w_out):
    for i in range(DEPTH):
        h = rmsnorm(x, norm_gains[i], NORM_EPS)
        j = i // N_MIXERS
        if i % N_MIXERS == 0:
            y = diff_attention_branch(h, attn_w_in[j], attn_lambda[j], attn_subln_gain[j],
                                      attn_w_out[j], rel_bias, i)
        else:
            y = pool_branch(h, pool_w_in[j], pool_w_group[j], pool_scale[j], pool_w_out[j])
        x = x + y
    return rmsnorm(x, final_norm_gain, NORM_EPS)
```

```python
import functools
import math

import jax
import jax.numpy as jnp
import numpy as np
from jax import lax
from jax.experimental import pallas as pl
from jax.experimental.pallas import tpu as pltpu

NUM_BUCKETS = 32
MAX_DISTANCE = 128
POOL_WINDOWS = (2, 4, 8, 16)
NORM_EPS = 1e-6
SUBLN_EPS = 1e-5
NEG_INF = -1e30
LOG2E = math.log2(math.e)

V7X_VMEM_LIMIT_BYTES = 56 * 1024 * 1024
LANES = 128
POOL_HALO = max(POOL_WINDOWS)

BF16 = jnp.bfloat16
F32 = jnp.float32


def _lambda_init(layer_idx):
    return 0.8 - 0.6 * math.exp(-0.3 * layer_idx)


def _bucket_thresholds():
    max_exact = NUM_BUCKETS // 2
    n = np.arange(0, 2 * MAX_DISTANCE, dtype=np.float64)
    large = max_exact + np.floor(
        np.log(np.maximum(n, max_exact) / max_exact) / math.log(MAX_DISTANCE / max_exact) * (NUM_BUCKETS - max_exact))
    bucket = np.where(n < max_exact, n, np.minimum(large, NUM_BUCKETS - 1)).astype(np.int64)
    return [int(np.argmax(bucket >= b)) for b in range(NUM_BUCKETS)]


def _tile(dim, target):
    t = min(dim, target)
    while dim % t:
        t -= LANES
    assert t > 0
    return t


def _params(n_axes):
    return pltpu.CompilerParams(dimension_semantics=("arbitrary",) * n_axes,
                                vmem_limit_bytes=V7X_VMEM_LIMIT_BYTES)


def _rmsnorm(x, gain, eps):
    inv = lax.rsqrt(jnp.mean(x * x, axis=-1, keepdims=True) + eps)
    return x * inv * gain


def _silu(x):
    return x * jax.nn.sigmoid(x)


def _attn_proj_kernel(x_ref, g_ref, w_ref, o_ref, h_ref, *, q_scale, tiles_per_part):
    j = pl.program_id(1)

    @pl.when(j == 0)
    def _():
        h_ref[...] = _rmsnorm(x_ref[...], g_ref[...], NORM_EPS).astype(BF16)

    acc = jnp.dot(h_ref[...], w_ref[...], preferred_element_type=F32)

    @pl.when(j < tiles_per_part)
    def _():
        o_ref[...] = (acc * q_scale).astype(o_ref.dtype)

    @pl.when((j >= tiles_per_part) & (j < 3 * tiles_per_part))
    def _():
        o_ref[...] = acc.astype(o_ref.dtype)

    @pl.when(j >= 3 * tiles_per_part)
    def _():
        o_ref[...] = _silu(acc).astype(o_ref.dtype)


def _attn_proj(x, gain, w, q_scale):
    s, d = x.shape
    n = w.shape[1]
    width = n // 4
    tm, tn = _tile(s, 512), _tile(width, 1024)
    kern = functools.partial(_attn_proj_kernel, q_scale=q_scale, tiles_per_part=width // tn)
    return pl.pallas_call(
        kern,
        grid=(s // tm, n // tn),
        in_specs=[pl.BlockSpec((tm, d), lambda i, j: (i, 0)),
                  pl.BlockSpec((1, d), lambda i, j: (0, 0)),
                  pl.BlockSpec((d, tn), lambda i, j: (0, j))],
        out_specs=pl.BlockSpec((tm, tn), lambda i, j: (i, j)),
        out_shape=jax.ShapeDtypeStruct((s, n), BF16),
        scratch_shapes=[pltpu.VMEM((tm, d), BF16)],
        compiler_params=_params(2),
        name="attn_proj",
    )(x, gain.reshape(1, d), w)


def _softmax_block(q, k, v, bias, m_ref, l_ref, acc_ref):
    s = lax.dot_general(q, k, (((1,), (1,)), ((), ())), preferred_element_type=F32)
    if bias is not None:
        s = s + bias
    m_old = m_ref[...]
    m_new = jnp.maximum(m_old, jnp.max(s, axis=-1, keepdims=True))
    alpha = jnp.exp2(m_old - m_new)
    p = jnp.exp2(s - m_new)
    l_ref[...] = alpha * l_ref[...] + jnp.sum(p, axis=-1, keepdims=True)
    acc_ref[...] = alpha * acc_ref[...] + jnp.dot(p.astype(BF16), v, preferred_element_type=F32)
    m_ref[...] = m_new


def _diff_attn_kernel(rb_ref, lam_ref, gain_ref, q_ref, k_ref, v_ref, z_ref, o_ref,
                      bias_ref, m_ref, l_ref, acc_ref, *, blk, head_dim, lam_init, thresholds):
    h = pl.program_id(0)
    i = pl.program_id(1)

    @pl.when(i == 0)
    def _():
        row = lax.broadcasted_iota(jnp.int32, (blk, blk), 0)
        col = lax.broadcasted_iota(jnp.int32, (blk, blk), 1)
        last = rb_ref[NUM_BUCKETS - 1, h]
        for t in range(2):
            dist = row - col + t * blk
            b = jnp.zeros((blk, blk), F32)
            for bucket in range(NUM_BUCKETS - 2, -1, -1):
                val = (rb_ref[bucket, h] - last) * LOG2E
                b = jnp.where(dist < thresholds[bucket + 1], val, b)
            if t == 0:
                b = jnp.where(dist >= 0, b, NEG_INF)
            bias_ref[t] = b

    m_ref[...] = jnp.full(m_ref.shape, -3e38, F32)
    l_ref[...] = jnp.zeros(l_ref.shape, F32)
    acc_ref[...] = jnp.zeros(acc_ref.shape, F32)

    q = q_ref[...]
    qs = (q[:, :head_dim], q[:, head_dim:])

    def block(j, bias):
        start = pl.multiple_of(j * blk, blk)
        kj = k_ref[pl.ds(start, blk), :]
        vj = v_ref[pl.ds(start, blk), :]
        for c in range(2):
            _softmax_block(qs[c], kj[:, c * head_dim:(c + 1) * head_dim], vj, bias,
                           m_ref.at[c], l_ref.at[c], acc_ref.at[c])

    def far_block(j, carry):
        block(j, None)
        return carry

    lax.fori_loop(0, jnp.maximum(i - 1, 0), far_block, 0)

    @pl.when(i >= 1)
    def _():
        block(i - 1, bias_ref[1])

    block(i, bias_ref[0])

    lp = lam_ref[...]
    lam = (jnp.exp(jnp.sum(lp[0:1] * lp[1:2], axis=-1, keepdims=True))
           - jnp.exp(jnp.sum(lp[2:3] * lp[3:4], axis=-1, keepdims=True)) + lam_init)
    o = acc_ref[0] * (1.0 / l_ref[0]) - lam * (acc_ref[1] * (1.0 / l_ref[1]))
    o = _rmsnorm(o, gain_ref[...], SUBLN_EPS) * (1.0 - lam_init)
    o_ref[...] = (o * z_ref[...].astype(F32)).astype(o_ref.dtype)


def _diff_attn(qkvz, rel_bias, lam_params, subln_gain, n_heads, lam_init):
    s = qkvz.shape[0]
    width = qkvz.shape[1] // 4
    v_dim = width // n_heads
    head_dim = v_dim // 2
    blk = _tile(s, 512)
    thresholds = _bucket_thresholds()
    assert thresholds[NUM_BUCKETS - 1] <= blk, "bias must be constant beyond the first sub-diagonal block"
    kern = functools.partial(_diff_attn_kernel, blk=blk, head_dim=head_dim, lam_init=lam_init,
                             thresholds=thresholds)
    return pl.pallas_call(
        kern,
        grid=(n_heads, s // blk),
        in_specs=[pl.BlockSpec(memory_space=pltpu.SMEM),
                  pl.BlockSpec(lam_params.shape, lambda h, i: (0, 0)),
                  pl.BlockSpec((1, v_dim), lambda h, i: (0, 0)),
                  pl.BlockSpec((blk, v_dim), lambda h, i: (i, h)),
                  pl.BlockSpec((s, v_dim), lambda h, i: (0, n_heads + h)),
                  pl.BlockSpec((s, v_dim), lambda h, i: (0, 2 * n_heads + h)),
                  pl.BlockSpec((blk, v_dim), lambda h, i: (i, 3 * n_heads + h))],
        out_specs=pl.BlockSpec((blk, v_dim), lambda h, i: (i, h)),
        out_shape=jax.ShapeDtypeStruct((s, width), BF16),
        scratch_shapes=[pltpu.VMEM((2, blk, blk), F32),
                        pltpu.VMEM((2, blk, 1), F32),
                        pltpu.VMEM((2, blk, 1), F32),
                        pltpu.VMEM((2, blk, v_dim), F32)],
        compiler_params=_params(2),
        name="diff_attn",
    )(rel_bias, lam_params, subln_gain.reshape(1, v_dim), qkvz, qkvz, qkvz, qkvz)


def _matmul_residual_kernel(a_ref, w_ref, r_ref, o_ref):
    o_ref[...] = r_ref[...] + jnp.dot(a_ref[...], w_ref[...], preferred_element_type=F32)


def _matmul_residual(a, w, resid, name):
    s, k = a.shape
    n = w.shape[1]
    tm, tn = _tile(s, 512), _tile(n, 512)
    return pl.pallas_call(
        _matmul_residual_kernel,
        grid=(s // tm, n // tn),
        in_specs=[pl.BlockSpec((tm, k), lambda i, j: (i, 0)),
                  pl.BlockSpec((k, tn), lambda i, j: (0, j)),
                  pl.BlockSpec((tm, tn), lambda i, j: (i, j))],
        out_specs=pl.BlockSpec((tm, tn), lambda i, j: (i, j)),
        out_shape=jax.ShapeDtypeStruct((s, n), F32),
        compiler_params=_params(2),
        name=name,
    )(a, w, resid)


def _pool_proj_kernel(x_ref, g_ref, w_ref, o_ref, h_ref, halo_ref, *, tm, n_u_tiles, tiles_per_group):
    i = pl.program_id(0)
    j = pl.program_id(1)

    @pl.when(j == 0)
    def _():
        h_ref[...] = _rmsnorm(x_ref[...], g_ref[...], NORM_EPS).astype(BF16)

    acc = jnp.dot(h_ref[...], w_ref[...], preferred_element_type=F32)

    @pl.when(j >= n_u_tiles)
    def _():
        o_ref[...] = _silu(acc).astype(o_ref.dtype)

    for g, window in enumerate(POOL_WINDOWS):
        @pl.when((j >= g * tiles_per_group) & (j < (g + 1) * tiles_per_group))
        def _(window=window):
            jj = jnp.minimum(j, n_u_tiles - 1)

            @pl.when(i == 0)
            def _():
                halo_ref[jj] = jnp.zeros(halo_ref.shape[1:], F32)

            total = jnp.concatenate([halo_ref[jj], acc], axis=0)
            shift = 1
            while shift < window:
                total = total + pltpu.roll(total, shift, axis=0)
                shift *= 2
            t = i * tm + lax.broadcasted_iota(jnp.int32, (tm, 1), 0)
            inv_cnt = 1.0 / jnp.minimum(t + 1, window).astype(F32)
            o_ref[...] = (total[POOL_HALO:] * inv_cnt - acc).astype(o_ref.dtype)
            halo_ref[jj] = acc[tm - POOL_HALO:]


def _pool_proj(x, gain, w, n_groups):
    s, d = x.shape
    n = w.shape[1]
    width = n // 2
    group_dim = width // n_groups
    tm, tn = _tile(s, 512), _tile(group_dim, 1024)
    n_u_tiles = width // tn
    kern = functools.partial(_pool_proj_kernel, tm=tm, n_u_tiles=n_u_tiles, tiles_per_group=group_dim // tn)
    return pl.pallas_call(
        kern,
        grid=(s // tm, n // tn),
        in_specs=[pl.BlockSpec((tm, d), lambda i, j: (i, 0)),
                  pl.BlockSpec((1, d), lambda i, j: (0, 0)),
                  pl.BlockSpec((d, tn), lambda i, j: (0, j))],
        out_specs=pl.BlockSpec((tm, tn), lambda i, j: (i, j)),
        out_shape=jax.ShapeDtypeStruct((s, n), BF16),
        scratch_shapes=[pltpu.VMEM((tm, d), BF16),
                        pltpu.VMEM((n_u_tiles, POOL_HALO, tn), F32)],
        compiler_params=_params(2),
        name="pool_proj",
    )(x, gain.reshape(1, d), w)


def _pool_group_kernel(p_ref, w_ref, s_ref, z_ref, o_ref):
    acc = jnp.dot(p_ref[...], w_ref[...], preferred_element_type=F32)
    o_ref[...] = (acc * s_ref[...] * z_ref[...].astype(F32)).astype(o_ref.dtype)


def _pool_group(pz, w_group, scale):
    s = pz.shape[0]
    n_groups, group_dim, _ = w_group.shape
    width = n_groups * group_dim
    tm, tn = _tile(s, 1024), _tile(group_dim, 1024)
    tpg = group_dim // tn
    return pl.pallas_call(
        _pool_group_kernel,
        grid=(s // tm, n_groups, tpg),
        in_specs=[pl.BlockSpec((tm, group_dim), lambda i, g, j: (i, g)),
                  pl.BlockSpec((None, group_dim, tn), lambda i, g, j: (g, 0, j)),
                  pl.BlockSpec((1, tn), lambda i, g, j: (0, g * tpg + j)),
                  pl.BlockSpec((tm, tn), lambda i, g, j: (i, (n_groups + g) * tpg + j))],
        out_specs=pl.BlockSpec((tm, tn), lambda i, g, j: (i, g * tpg + j)),
        out_shape=jax.ShapeDtypeStruct((s, width), BF16),
        compiler_params=_params(3),
        name="pool_group",
    )(pz, w_group, scale.reshape(1, width), pz)


def _final_norm_kernel(x_ref, g_ref, o_ref):
    o_ref[...] = _rmsnorm(x_ref[...], g_ref[...], NORM_EPS)


def _final_norm(x, gain):
    s, d = x.shape
    tm = _tile(s, 512)
    return pl.pallas_call(
        _final_norm_kernel,
        grid=(s // tm,),
        in_specs=[pl.BlockSpec((tm, d), lambda i: (i, 0)),
                  pl.BlockSpec((1, d), lambda i: (0, 0))],
        out_specs=pl.BlockSpec((tm, d), lambda i: (i, 0)),
        out_shape=jax.ShapeDtypeStruct((s, d), F32),
        compiler_params=_params(1),
        name="final_norm",
    )(x, gain.reshape(1, d))


def kernel(x, norm_gains, final_norm_gain, rel_bias, attn_w_in, attn_lambda, attn_subln_gain, attn_w_out,
           pool_w_in, pool_w_group, pool_scale, pool_w_out):
    b, s, d = x.shape
    assert b == 1, "attention and pooling tiles assume one sequence"
    assert norm_gains.shape[0] == 2 and attn_w_in.shape[0] == 1 and pool_w_in.shape[0] == 1
    n_heads = rel_bias.shape[1]
    head_dim = attn_w_in.shape[2] // (8 * n_heads)
    xs = x.reshape(s, d)

    qkvz = _attn_proj(xs, norm_gains[0], attn_w_in[0].astype(BF16), head_dim ** -0.5 * LOG2E)
    og = _diff_attn(qkvz, rel_bias, attn_lambda[0], attn_subln_gain[0], n_heads, _lambda_init(0))
    x1 = _matmul_residual(og, attn_w_out[0].astype(BF16), xs, "attn_out")

    pz = _pool_proj(x1, norm_gains[1], pool_w_in[0].astype(BF16), pool_w_group.shape[1])
    mixed = _pool_group(pz, pool_w_group[0].astype(BF16), pool_scale[0])
    x2 = _matmul_residual(mixed, pool_w_out[0].astype(BF16), x1, "pool_out")

    return _final_norm(x2, final_norm_gain).reshape(b, s, d)
```

```python
import functools
import math

import jax
import jax.numpy as jnp
import numpy as np
from jax import lax
from jax.experimental import pallas as pl
from jax.experimental.pallas import tpu as pltpu

NUM_BUCKETS = 32
MAX_DISTANCE = 128
POOL_WINDOWS = (2, 4, 8, 16)
NORM_EPS = 1e-6
SUBLN_EPS = 1e-5
NEG_INF = -1e30
LOG2E = math.log2(math.e)

V7X_VMEM_LIMIT_BYTES = 56 * 1024 * 1024
LANES = 128
POOL_HALO = max(POOL_WINDOWS)

BF16 = jnp.bfloat16
F32 = jnp.float32


def _lambda_init(layer_idx):
    return 0.8 - 0.6 * math.exp(-0.3 * layer_idx)


def _bucket_thresholds():
    max_exact = NUM_BUCKETS // 2
    n = np.arange(0, 2 * MAX_DISTANCE, dtype=np.float64)
    large = max_exact + np.floor(
        np.log(np.maximum(n, max_exact) / max_exact) / math.log(MAX_DISTANCE / max_exact) * (NUM_BUCKETS - max_exact))
    bucket = np.where(n < max_exact, n, np.minimum(large, NUM_BUCKETS - 1)).astype(np.int64)
    return [int(np.argmax(bucket >= b)) for b in range(NUM_BUCKETS)]


def _tile(dim, target):
    t = min(dim, target)
    while dim % t:
        t -= LANES
    assert t > 0
    return t


def _params(n_axes):
    return pltpu.CompilerParams(dimension_semantics=("arbitrary",) * n_axes,
                                vmem_limit_bytes=V7X_VMEM_LIMIT_BYTES)


def _rmsnorm(x, gain, eps):
    inv = lax.rsqrt(jnp.mean(x * x, axis=-1, keepdims=True) + eps)
    return x * inv * gain


def _silu(x):
    return x * jax.nn.sigmoid(x)


def _attn_proj_kernel(x_ref, g_ref, w_ref, o_ref, h_ref, *, q_scale, tiles_per_part):
    j = pl.program_id(1)

    @pl.when(j == 0)
    def _():
        h_ref[...] = _rmsnorm(x_ref[...], g_ref[...], NORM_EPS).astype(BF16)

    acc = jnp.dot(h_ref[...], w_ref[...], preferred_element_type=F32)

    @pl.when(j < tiles_per_part)
    def _():
        o_ref[...] = (acc * q_scale).astype(o_ref.dtype)

    @pl.when((j >= tiles_per_part) & (j < 3 * tiles_per_part))
    def _():
        o_ref[...] = acc.astype(o_ref.dtype)

    @pl.when(j >= 3 * tiles_per_part)
    def _():
        o_ref[...] = _silu(acc).astype(o_ref.dtype)


def _attn_proj(x, gain, w, q_scale):
    s, d = x.shape
    n = w.shape[1]
    width = n // 4
    tm, tn = _tile(s, 512), _tile(width, 1024)
    kern = functools.partial(_attn_proj_kernel, q_scale=q_scale, tiles_per_part=width // tn)
    return pl.pallas_call(
        kern,
        grid=(s // tm, n // tn),
        in_specs=[pl.BlockSpec((tm, d), lambda i, j: (i, 0)),
                  pl.BlockSpec((1, d), lambda i, j: (0, 0)),
                  pl.BlockSpec((d, tn), lambda i, j: (0, j))],
        out_specs=pl.BlockSpec((tm, tn), lambda i, j: (i, j)),
        out_shape=jax.ShapeDtypeStruct((s, n), BF16),
        scratch_shapes=[pltpu.VMEM((tm, d), BF16)],
        compiler_params=_params(2),
        name="attn_proj",
    )(x, gain.reshape(1, d), w)


_NT_DIMS = (((1,), (1,)), ((), ()))


def _diff_attn_kernel(rb_ref, lam_ref, gain_ref, q_ref, k_ref, v_ref, z_ref, o_ref,
                      bias_ref, sa_ref, sb_ref, m_ref, l_ref, acc_ref, *, blk, head_dim, lam_init, thresholds):
    h = pl.program_id(0)
    i = pl.program_id(1)
    lane_tiles = blk // LANES
    v_lane_tiles = acc_ref.shape[-1] // LANES

    @pl.when(i == 0)
    def _():
        row = lax.broadcasted_iota(jnp.int32, (blk, blk), 0)
        col = lax.broadcasted_iota(jnp.int32, (blk, blk), 1)
        last = rb_ref[NUM_BUCKETS - 1, h]
        for t in range(2):
            dist = row - col + t * blk
            b = jnp.zeros((blk, blk), F32)
            for bucket in range(NUM_BUCKETS - 2, -1, -1):
                val = (rb_ref[bucket, h] - last) * LOG2E
                b = jnp.where(dist < thresholds[bucket + 1], val, b)
            if t == 0:
                b = jnp.where(dist >= 0, b, NEG_INF)
            bias_ref[t] = b

    m_ref[...] = jnp.full(m_ref.shape, -3e38, F32)
    l_ref[...] = jnp.zeros(l_ref.shape, F32)
    acc_ref[...] = jnp.zeros(acc_ref.shape, F32)

    q = q_ref[...]
    qs = (q[:, :head_dim], q[:, head_dim:])

    def logits(j, bias, dst_ref):
        kj = k_ref[pl.ds(pl.multiple_of(j * blk, blk), blk), :]
        for c in range(2):
            s = lax.dot_general(qs[c], kj[:, c * head_dim:(c + 1) * head_dim], _NT_DIMS,
                                preferred_element_type=F32)
            dst_ref[c] = s if bias is None else s + bias

    def accumulate(j, src_ref):
        vj = v_ref[pl.ds(pl.multiple_of(j * blk, blk), blk), :]
        for c in range(2):
            s = src_ref[c]
            tiles = [s[:, t * LANES:(t + 1) * LANES] for t in range(lane_tiles)]
            m_old = m_ref[c]
            m_new = jnp.maximum(m_old, jnp.max(functools.reduce(jnp.maximum, tiles), axis=-1, keepdims=True))
            alpha = jnp.exp2(m_old - m_new)
            ps = [jnp.exp2(t - m_new) for t in tiles]
            l_ref[c] = alpha * l_ref[c] + functools.reduce(jnp.add, ps)
            pv = jnp.dot(jnp.concatenate(ps, axis=1).astype(BF16), vj, preferred_element_type=F32)
            acc_ref[c] = jnp.concatenate([alpha] * v_lane_tiles, axis=1) * acc_ref[c] + pv
            m_ref[c] = m_new

    logits(i, bias_ref[0], sa_ref)

    @pl.when(i == 0)
    def _():
        accumulate(i, sa_ref)

    @pl.when(i >= 1)
    def _():
        logits(i - 1, bias_ref[1], sb_ref)
        accumulate(i, sa_ref)
        n_far = i - 1

        def far_pair(t, carry):
            j = i - 2 - 2 * t
            logits(j, None, sa_ref)
            accumulate(j + 1, sb_ref)
            logits(j - 1, None, sb_ref)
            accumulate(j, sa_ref)
            return carry

        lax.fori_loop(0, n_far // 2, far_pair, 0)

        @pl.when(n_far % 2 == 1)
        def _():
            logits(0, None, sa_ref)
            accumulate(1, sb_ref)
            accumulate(0, sa_ref)

        @pl.when(n_far % 2 == 0)
        def _():
            accumulate(0, sb_ref)

    lp = lam_ref[...]
    lam = (jnp.exp(jnp.sum(lp[0:1] * lp[1:2], axis=-1, keepdims=True))
           - jnp.exp(jnp.sum(lp[2:3] * lp[3:4], axis=-1, keepdims=True)) + lam_init)
    inv_l = [1.0 / jnp.sum(l_ref[c], axis=-1, keepdims=True) for c in range(2)]
    o = acc_ref[0] * inv_l[0] - lam * (acc_ref[1] * inv_l[1])
    o = _rmsnorm(o, gain_ref[...], SUBLN_EPS) * (1.0 - lam_init)
    o_ref[...] = (o * z_ref[...].astype(F32)).astype(o_ref.dtype)


def _diff_attn(qkvz, rel_bias, lam_params, subln_gain, n_heads, lam_init):
    s = qkvz.shape[0]
    width = qkvz.shape[1] // 4
    v_dim = width // n_heads
    head_dim = v_dim // 2
    blk = _tile(s, 512)
    thresholds = _bucket_thresholds()
    assert thresholds[NUM_BUCKETS - 1] <= blk, "bias must be constant beyond the first sub-diagonal block"
    kern = functools.partial(_diff_attn_kernel, blk=blk, head_dim=head_dim, lam_init=lam_init,
                             thresholds=thresholds)
    return pl.pallas_call(
        kern,
        grid=(n_heads, s // blk),
        in_specs=[pl.BlockSpec(memory_space=pltpu.SMEM),
                  pl.BlockSpec(lam_params.shape, lambda h, i: (0, 0)),
                  pl.BlockSpec((1, v_dim), lambda h, i: (0, 0)),
                  pl.BlockSpec((blk, v_dim), lambda h, i: (i, h)),
                  pl.BlockSpec((s, v_dim), lambda h, i: (0, n_heads + h)),
                  pl.BlockSpec((s, v_dim), lambda h, i: (0, 2 * n_heads + h)),
                  pl.BlockSpec((blk, v_dim), lambda h, i: (i, 3 * n_heads + h))],
        out_specs=pl.BlockSpec((blk, v_dim), lambda h, i: (i, h)),
        out_shape=jax.ShapeDtypeStruct((s, width), BF16),
        scratch_shapes=[pltpu.VMEM((2, blk, blk), F32),
                        pltpu.VMEM((2, blk, blk), F32),
                        pltpu.VMEM((2, blk, blk), F32),
                        pltpu.VMEM((2, blk, LANES), F32),
                        pltpu.VMEM((2, blk, LANES), F32),
                        pltpu.VMEM((2, blk, v_dim), F32)],
        compiler_params=_params(2),
        name="diff_attn",
    )(rel_bias, lam_params, subln_gain.reshape(1, v_dim), qkvz, qkvz, qkvz, qkvz)


def _matmul_residual_kernel(a_ref, w_ref, r_ref, o_ref):
    o_ref[...] = r_ref[...] + jnp.dot(a_ref[...], w_ref[...], preferred_element_type=F32)


def _matmul_residual(a, w, resid, name):
    s, k = a.shape
    n = w.shape[1]
    tm, tn = _tile(s, 512), _tile(n, 512)
    return pl.pallas_call(
        _matmul_residual_kernel,
        grid=(s // tm, n // tn),
        in_specs=[pl.BlockSpec((tm, k), lambda i, j: (i, 0)),
                  pl.BlockSpec((k, tn), lambda i, j: (0, j)),
                  pl.BlockSpec((tm, tn), lambda i, j: (i, j))],
        out_specs=pl.BlockSpec((tm, tn), lambda i, j: (i, j)),
        out_shape=jax.ShapeDtypeStruct((s, n), F32),
        compiler_params=_params(2),
        name=name,
    )(a, w, resid)


def _pool_proj_kernel(x_ref, g_ref, w_ref, o_ref, h_ref, halo_ref, *, tm, n_u_tiles, tiles_per_group):
    i = pl.program_id(0)
    j = pl.program_id(1)

    @pl.when(j == 0)
    def _():
        h_ref[...] = _rmsnorm(x_ref[...], g_ref[...], NORM_EPS).astype(BF16)

    acc = jnp.dot(h_ref[...], w_ref[...], preferred_element_type=F32)

    @pl.when(j >= n_u_tiles)
    def _():
        o_ref[...] = _silu(acc).astype(o_ref.dtype)

    for g, window in enumerate(POOL_WINDOWS):
        @pl.when((j >= g * tiles_per_group) & (j < (g + 1) * tiles_per_group))
        def _(window=window):
            jj = jnp.minimum(j, n_u_tiles - 1)

            @pl.when(i == 0)
            def _():
                halo_ref[jj] = jnp.zeros(halo_ref.shape[1:], F32)

            total = jnp.concatenate([halo_ref[jj], acc], axis=0)
            shift = 1
            while shift < window:
                total = total + pltpu.roll(total, shift, axis=0)
                shift *= 2
            t = i * tm + lax.broadcasted_iota(jnp.int32, (tm, 1), 0)
            inv_cnt = 1.0 / jnp.minimum(t + 1, window).astype(F32)
            o_ref[...] = (total[POOL_HALO:] * inv_cnt - acc).astype(o_ref.dtype)
            halo_ref[jj] = acc[tm - POOL_HALO:]


def _pool_proj(x, gain, w, n_groups):
    s, d = x.shape
    n = w.shape[1]
    width = n // 2
    group_dim = width // n_groups
    tm, tn = _tile(s, 512), _tile(group_dim, 1024)
    n_u_tiles = width // tn
    kern = functools.partial(_pool_proj_kernel, tm=tm, n_u_tiles=n_u_tiles, tiles_per_group=group_dim // tn)
    return pl.pallas_call(
        kern,
        grid=(s // tm, n // tn),
        in_specs=[pl.BlockSpec((tm, d), lambda i, j: (i, 0)),
                  pl.BlockSpec((1, d), lambda i, j: (0, 0)),
                  pl.BlockSpec((d, tn), lambda i, j: (0, j))],
        out_specs=pl.BlockSpec((tm, tn), lambda i, j: (i, j)),
        out_shape=jax.ShapeDtypeStruct((s, n), BF16),
        scratch_shapes=[pltpu.VMEM((tm, d), BF16),
                        pltpu.VMEM((n_u_tiles, POOL_HALO, tn), F32)],
        compiler_params=_params(2),
        name="pool_proj",
    )(x, gain.reshape(1, d), w)


def _pool_group_kernel(p_ref, w_ref, s_ref, z_ref, o_ref):
    acc = jnp.dot(p_ref[...], w_ref[...], preferred_element_type=F32)
    o_ref[...] = (acc * s_ref[...] * z_ref[...].astype(F32)).astype(o_ref.dtype)


def _pool_group(pz, w_group, scale):
    s = pz.shape[0]
    n_groups, group_dim, _ = w_group.shape
    width = n_groups * group_dim
    tm, tn = _tile(s, 1024), _tile(group_dim, 1024)
    tpg = group_dim // tn
    return pl.pallas_call(
        _pool_group_kernel,
        grid=(s // tm, n_groups, tpg),
        in_specs=[pl.BlockSpec((tm, group_dim), lambda i, g, j: (i, g)),
                  pl.BlockSpec((None, group_dim, tn), lambda i, g, j: (g, 0, j)),
                  pl.BlockSpec((1, tn), lambda i, g, j: (0, g * tpg + j)),
                  pl.BlockSpec((tm, tn), lambda i, g, j: (i, (n_groups + g) * tpg + j))],
        out_specs=pl.BlockSpec((tm, tn), lambda i, g, j: (i, g * tpg + j)),
        out_shape=jax.ShapeDtypeStruct((s, width), BF16),
        compiler_params=_params(3),
        name="pool_group",
    )(pz, w_group, scale.reshape(1, width), pz)


def _final_norm_kernel(x_ref, g_ref, o_ref):
    o_ref[...] = _rmsnorm(x_ref[...], g_ref[...], NORM_EPS)


def _final_norm(x, gain):
    s, d = x.shape
    tm = _tile(s, 512)
    return pl.pallas_call(
        _final_norm_kernel,
        grid=(s // tm,),
        in_specs=[pl.BlockSpec((tm, d), lambda i: (i, 0)),
                  pl.BlockSpec((1, d), lambda i: (0, 0))],
        out_specs=pl.BlockSpec((tm, d), lambda i: (i, 0)),
        out_shape=jax.ShapeDtypeStruct((s, d), F32),
        compiler_params=_params(1),
        name="final_norm",
    )(x, gain.reshape(1, d))


def kernel(x, norm_gains, final_norm_gain, rel_bias, attn_w_in, attn_lambda, attn_subln_gain, attn_w_out,
           pool_w_in, pool_w_group, pool_scale, pool_w_out):
    b, s, d = x.shape
    assert b == 1, "attention and pooling tiles assume one sequence"
    assert norm_gains.shape[0] == 2 and attn_w_in.shape[0] == 1 and pool_w_in.shape[0] == 1
    n_heads = rel_bias.shape[1]
    head_dim = attn_w_in.shape[2] // (8 * n_heads)
    xs = x.reshape(s, d)

    qkvz = _attn_proj(xs, norm_gains[0], attn_w_in[0].astype(BF16), head_dim ** -0.5 * LOG2E)
    og = _diff_attn(qkvz, rel_bias, attn_lambda[0], attn_subln_gain[0], n_heads, _lambda_init(0))
    x1 = _matmul_residual(og, attn_w_out[0].astype(BF16), xs, "attn_out")

    pz = _pool_proj(x1, norm_gains[1], pool_w_in[0].astype(BF16), pool_w_group.shape[1])
    mixed = _pool_group(pz, pool_w_group[0].astype(BF16), pool_scale[0])
    x2 = _matmul_residual(mixed, pool_w_out[0].astype(BF16), x1, "pool_out")

    return _final_norm(x2, final_norm_gain).reshape(b, s, d)
```

```python
import functools
import math

import jax
import jax.numpy as jnp
import numpy as np
from jax import lax
from jax.experimental import pallas as pl
from jax.experimental.pallas import tpu as pltpu

NUM_BUCKETS = 32
MAX_DISTANCE = 128
POOL_WINDOWS = (2, 4, 8, 16)
assert all(w == 2 << k for k, w in enumerate(POOL_WINDOWS)), "pooling doubles a running window sum"
NORM_EPS = 1e-6
SUBLN_EPS = 1e-5
NEG_INF = -1e30
LOG2E = math.log2(math.e)

V7X_VMEM_LIMIT_BYTES = 56 * 1024 * 1024
LANES = 128
V7X_MXU_WIDTH = 256
POOL_HALO = max(POOL_WINDOWS)

BF16 = jnp.bfloat16
F32 = jnp.float32


def _lambda_init(layer_idx):
    return 0.8 - 0.6 * math.exp(-0.3 * layer_idx)


def _bucket_thresholds():
    max_exact = NUM_BUCKETS // 2
    n = np.arange(0, 2 * MAX_DISTANCE, dtype=np.float64)
    large = max_exact + np.floor(
        np.log(np.maximum(n, max_exact) / max_exact) / math.log(MAX_DISTANCE / max_exact) * (NUM_BUCKETS - max_exact))
    bucket = np.where(n < max_exact, n, np.minimum(large, NUM_BUCKETS - 1)).astype(np.int64)
    return [int(np.argmax(bucket >= b)) for b in range(NUM_BUCKETS)]


def _tile(dim, target):
    t = min(dim, target)
    while dim % t:
        t -= LANES
    assert t > 0
    return t


def _params(n_axes):
    return pltpu.CompilerParams(dimension_semantics=("arbitrary",) * n_axes,
                                vmem_limit_bytes=V7X_VMEM_LIMIT_BYTES)


def _rmsnorm(x, gain, eps):
    inv = lax.rsqrt(jnp.mean(x * x, axis=-1, keepdims=True) + eps)
    return x * inv * gain


def _silu(x):
    return x * jax.nn.sigmoid(x)


def _col_chunks(n):
    step = V7X_MXU_WIDTH if n % V7X_MXU_WIDTH == 0 else n
    return [slice(c, c + step) for c in range(0, n, step)]


def _attn_proj_kernel(x_ref, g_ref, w_ref, o_ref, h_ref, raw_ref, *, n_col_tiles, q_scale, tiles_per_part):
    t = pl.program_id(0)
    n_tiles = pl.num_programs(0) - 1

    @pl.when(t == 0)
    def _():
        raw_ref[...] = jnp.zeros(raw_ref.shape, F32)

    @pl.when(jnp.minimum(t, n_tiles - 1) % n_col_tiles == 0)
    def _():
        h_ref[...] = _rmsnorm(x_ref[...], g_ref[...], NORM_EPS).astype(BF16)

    j_prev = jnp.maximum(t - 1, 0) % n_col_tiles
    scale = jnp.where(j_prev < tiles_per_part, q_scale, 1.0)
    is_gate = j_prev >= 3 * tiles_per_part
    for cols in _col_chunks(o_ref.shape[1]):
        acc = raw_ref[:, cols]
        o_ref[:, cols] = jnp.where(is_gate, _silu(acc), acc * scale).astype(o_ref.dtype)
    for cols in _col_chunks(o_ref.shape[1]):
        raw_ref[:, cols] = jnp.dot(h_ref[...], w_ref[:, cols], preferred_element_type=F32)


def _attn_proj(x, gain, w, q_scale):
    s, d = x.shape
    n = w.shape[1]
    width = n // 4
    tm, tn = _tile(s, 512), _tile(width, 1024)
    n_col_tiles = n // tn
    n_tiles = (s // tm) * n_col_tiles
    kern = functools.partial(_attn_proj_kernel, n_col_tiles=n_col_tiles, q_scale=q_scale,
                             tiles_per_part=width // tn)

    def cur(t):
        return jnp.minimum(t, n_tiles - 1)

    def prev(t):
        return jnp.maximum(t - 1, 0)

    return pl.pallas_call(
        kern,
        grid=(n_tiles + 1,),
        in_specs=[pl.BlockSpec((tm, d), lambda t: (cur(t) // n_col_tiles, 0)),
                  pl.BlockSpec((1, d), lambda t: (0, 0)),
                  pl.BlockSpec((d, tn), lambda t: (0, cur(t) % n_col_tiles))],
        out_specs=pl.BlockSpec((tm, tn), lambda t: (prev(t) // n_col_tiles, prev(t) % n_col_tiles)),
        out_shape=jax.ShapeDtypeStruct((s, n), BF16),
        scratch_shapes=[pltpu.VMEM((tm, d), BF16), pltpu.VMEM((tm, tn), F32)],
        compiler_params=_params(1),
        name="attn_proj",
    )(x, gain.reshape(1, d), w)


_NT_DIMS = (((1,), (1,)), ((), ()))


def _diff_attn_kernel(rb_ref, lam_ref, gain_ref, q_ref, k_ref, v_ref, z_ref, o_ref,
                      bias_ref, sa_ref, sb_ref, ma_ref, mb_ref, m_ref, l_ref, acc_ref,
                      *, blk, head_dim, lam_init, thresholds):
    h = pl.program_id(0)
    i = pl.program_id(1)
    lane_tiles = blk // LANES
    v_lane_tiles = acc_ref.shape[-1] // LANES

    @pl.when(i == 0)
    def _():
        row = lax.broadcasted_iota(jnp.int32, (blk, blk), 0)
        col = lax.broadcasted_iota(jnp.int32, (blk, blk), 1)
        last = rb_ref[NUM_BUCKETS - 1, h]
        for t in range(2):
            dist = row - col + t * blk
            b = jnp.zeros((blk, blk), F32)
            for bucket in range(NUM_BUCKETS - 2, -1, -1):
                val = (rb_ref[bucket, h] - last) * LOG2E
                b = jnp.where(dist < thresholds[bucket + 1], val, b)
            if t == 0:
                b = jnp.where(dist >= 0, b, NEG_INF)
            bias_ref[t] = b

    m_ref[...] = jnp.full(m_ref.shape, -3e38, F32)
    l_ref[...] = jnp.zeros(l_ref.shape, F32)
    acc_ref[...] = jnp.zeros(acc_ref.shape, F32)

    q = q_ref[...]
    qs = (q[:, :head_dim], q[:, head_dim:])

    def lane_tiles_of(s):
        return [s[:, t * LANES:(t + 1) * LANES] for t in range(lane_tiles)]

    def logits(j, bias, dst):
        s_dst, max_dst = dst
        kj = k_ref[pl.ds(pl.multiple_of(j * blk, blk), blk), :]
        for c in range(2):
            s = lax.dot_general(qs[c], kj[:, c * head_dim:(c + 1) * head_dim], _NT_DIMS,
                                preferred_element_type=F32)
            if bias is not None:
                s = s + bias
            s_dst[c] = s
            row_max = jnp.max(functools.reduce(jnp.maximum, lane_tiles_of(s)), axis=-1, keepdims=True)
            max_dst[c] = jnp.broadcast_to(row_max, (blk, LANES))

    def accumulate(j, src):
        s_src, max_src = src
        vj = v_ref[pl.ds(pl.multiple_of(j * blk, blk), blk), :]
        for c in range(2):
            tiles = lane_tiles_of(s_src[c])
            m_old = m_ref[c]
            m_new = jnp.maximum(m_old, max_src[c])
            alpha = jnp.exp2(m_old - m_new)
            ps = [jnp.exp2(t - m_new) for t in tiles]
            l_ref[c] = alpha * l_ref[c] + functools.reduce(jnp.add, ps)
            pv = jnp.dot(jnp.concatenate(ps, axis=1).astype(BF16), vj, preferred_element_type=F32)
            acc_ref[c] = jnp.concatenate([alpha] * v_lane_tiles, axis=1) * acc_ref[c] + pv
            m_ref[c] = m_new

    buf_a = (sa_ref, ma_ref)
    buf_b = (sb_ref, mb_ref)
    logits(i, bias_ref[0], buf_a)

    @pl.when(i == 0)
    def _():
        accumulate(i, buf_a)

    @pl.when(i >= 1)
    def _():
        logits(i - 1, bias_ref[1], buf_b)
        accumulate(i, buf_a)
        n_far = i - 1

        def far_pair(t, carry):
            j = i - 2 - 2 * t
            logits(j, None, buf_a)
            accumulate(j + 1, buf_b)
            logits(j - 1, None, buf_b)
            accumulate(j, buf_a)
            return carry

        lax.fori_loop(0, n_far // 2, far_pair, 0)

        @pl.when(n_far % 2 == 1)
        def _():
            logits(0, None, buf_a)
            accumulate(1, buf_b)
            accumulate(0, buf_a)

        @pl.when(n_far % 2 == 0)
        def _():
            accumulate(0, buf_b)

    lp = lam_ref[...]
    lam = (jnp.exp(jnp.sum(lp[0:1] * lp[1:2], axis=-1, keepdims=True))
           - jnp.exp(jnp.sum(lp[2:3] * lp[3:4], axis=-1, keepdims=True)) + lam_init)
    inv_l = [1.0 / jnp.sum(l_ref[c], axis=-1, keepdims=True) for c in range(2)]
    o = acc_ref[0] * inv_l[0] - lam * (acc_ref[1] * inv_l[1])
    o = _rmsnorm(o, gain_ref[...], SUBLN_EPS) * (1.0 - lam_init)
    o_ref[...] = (o * z_ref[...].astype(F32)).astype(o_ref.dtype)


def _diff_attn(qkvz, rel_bias, lam_params, subln_gain, n_heads, lam_init):
    s = qkvz.shape[0]
    width = qkvz.shape[1] // 4
    v_dim = width // n_heads
    head_dim = v_dim // 2
    blk = _tile(s, 512)
    thresholds = _bucket_thresholds()
    assert thresholds[NUM_BUCKETS - 1] <= blk, "bias must be constant beyond the first sub-diagonal block"
    kern = functools.partial(_diff_attn_kernel, blk=blk, head_dim=head_dim, lam_init=lam_init,
                             thresholds=thresholds)
    return pl.pallas_call(
        kern,
        grid=(n_heads, s // blk),
        in_specs=[pl.BlockSpec(memory_space=pltpu.SMEM),
                  pl.BlockSpec(lam_params.shape, lambda h, i: (0, 0)),
                  pl.BlockSpec((1, v_dim), lambda h, i: (0, 0)),
                  pl.BlockSpec((blk, v_dim), lambda h, i: (i, h)),
                  pl.BlockSpec((s, v_dim), lambda h, i: (0, n_heads + h)),
                  pl.BlockSpec((s, v_dim), lambda h, i: (0, 2 * n_heads + h)),
                  pl.BlockSpec((blk, v_dim), lambda h, i: (i, 3 * n_heads + h))],
        out_specs=pl.BlockSpec((blk, v_dim), lambda h, i: (i, h)),
        out_shape=jax.ShapeDtypeStruct((s, width), BF16),
        scratch_shapes=[pltpu.VMEM((2, blk, blk), F32),
                        pltpu.VMEM((2, blk, blk), F32),
                        pltpu.VMEM((2, blk, blk), F32),
                        pltpu.VMEM((2, blk, LANES), F32),
                        pltpu.VMEM((2, blk, LANES), F32),
                        pltpu.VMEM((2, blk, LANES), F32),
                        pltpu.VMEM((2, blk, LANES), F32),
                        pltpu.VMEM((2, blk, v_dim), F32)],
        compiler_params=_params(2),
        name="diff_attn",
    )(rel_bias, lam_params, subln_gain.reshape(1, v_dim), qkvz, qkvz, qkvz, qkvz)


def _matmul_residual_kernel(a_ref, w_ref, r_ref, o_ref):
    for cols in _col_chunks(o_ref.shape[1]):
        o_ref[:, cols] = r_ref[:, cols] + jnp.dot(a_ref[...], w_ref[:, cols], preferred_element_type=F32)


def _matmul_residual(a, w, resid, name):
    s, k = a.shape
    n = w.shape[1]
    tm, tn = _tile(s, 512), _tile(n, 512)
    return pl.pallas_call(
        _matmul_residual_kernel,
        grid=(s // tm, n // tn),
        in_specs=[pl.BlockSpec((tm, k), lambda i, j: (i, 0)),
                  pl.BlockSpec((k, tn), lambda i, j: (0, j)),
                  pl.BlockSpec((tm, tn), lambda i, j: (i, j))],
        out_specs=pl.BlockSpec((tm, tn), lambda i, j: (i, j)),
        out_shape=jax.ShapeDtypeStruct((s, n), F32),
        compiler_params=_params(2),
        name=name,
    )(a, w, resid)


def _pool_proj_kernel(x_ref, g_ref, w_ref, o_ref, h_ref, halo_ref, *, tm, n_u_tiles, tiles_per_group):
    i = pl.program_id(0)
    j = pl.program_id(1)

    @pl.when(j == 0)
    def _():
        h_ref[...] = _rmsnorm(x_ref[...], g_ref[...], NORM_EPS).astype(BF16)

    acc = jnp.dot(h_ref[...], w_ref[...], preferred_element_type=F32)

    @pl.when(j >= n_u_tiles)
    def _():
        o_ref[...] = _silu(acc).astype(o_ref.dtype)

    for g, window in enumerate(POOL_WINDOWS):
        @pl.when((j >= g * tiles_per_group) & (j < (g + 1) * tiles_per_group))
        def _(window=window):
            jj = jnp.minimum(j, n_u_tiles - 1)

            @pl.when(i == 0)
            def _():
                halo_ref[jj] = jnp.zeros(halo_ref.shape[1:], F32)

            total = jnp.concatenate([halo_ref[jj], acc], axis=0)
            shift = 1
            while shift < window:
                total = total + pltpu.roll(total, shift, axis=0)
                shift *= 2
            t = i * tm + lax.broadcasted_iota(jnp.int32, (tm, 1), 0)
            inv_cnt = 1.0 / jnp.minimum(t + 1, window).astype(F32)
            o_ref[...] = (total[POOL_HALO:] * inv_cnt - acc).astype(o_ref.dtype)
            halo_ref[jj] = acc[tm - POOL_HALO:]


def _pool_proj(x, gain, w, n_groups):
    s, d = x.shape
    n = w.shape[1]
    width = n // 2
    group_dim = width // n_groups
    tm, tn = _tile(s, 512), _tile(group_dim, 1024)
    n_u_tiles = width // tn
    kern = functools.partial(_pool_proj_kernel, tm=tm, n_u_tiles=n_u_tiles, tiles_per_group=group_dim // tn)
    return pl.pallas_call(
        kern,
        grid=(s // tm, n // tn),
        in_specs=[pl.BlockSpec((tm, d), lambda i, j: (i, 0)),
                  pl.BlockSpec((1, d), lambda i, j: (0, 0)),
                  pl.BlockSpec((d, tn), lambda i, j: (0, j))],
        out_specs=pl.BlockSpec((tm, tn), lambda i, j: (i, j)),
        out_shape=jax.ShapeDtypeStruct((s, n), BF16),
        scratch_shapes=[pltpu.VMEM((tm, d), BF16),
                        pltpu.VMEM((n_u_tiles, POOL_HALO, tn), F32)],
        compiler_params=_params(2),
        name="pool_proj",
    )(x, gain.reshape(1, d), w)


def _pool_group_kernel(p_ref, w_ref, s_ref, z_ref, o_ref):
    for cols in _col_chunks(o_ref.shape[1]):
        acc = jnp.dot(p_ref[...], w_ref[:, cols], preferred_element_type=F32)
        o_ref[:, cols] = (acc * s_ref[:, cols] * z_ref[:, cols].astype(F32)).astype(o_ref.dtype)


def _pool_group(pz, w_group, scale):
    s = pz.shape[0]
    n_groups, group_dim, _ = w_group.shape
    width = n_groups * group_dim
    tm, tn = _tile(s, 1024), _tile(group_dim, 1024)
    tpg = group_dim // tn
    return pl.pallas_call(
        _pool_group_kernel,
        grid=(s // tm, n_groups, tpg),
        in_specs=[pl.BlockSpec((tm, group_dim), lambda i, g, j: (i, g)),
                  pl.BlockSpec((None, group_dim, tn), lambda i, g, j: (g, 0, j)),
                  pl.BlockSpec((1, tn), lambda i, g, j: (0, g * tpg + j)),
                  pl.BlockSpec((tm, tn), lambda i, g, j: (i, (n_groups + g) * tpg + j))],
        out_specs=pl.BlockSpec((tm, tn), lambda i, g, j: (i, g * tpg + j)),
        out_shape=jax.ShapeDtypeStruct((s, width), BF16),
        compiler_params=_params(3),
        name="pool_group",
    )(pz, w_group, scale.reshape(1, width), pz)


def _final_norm_kernel(x_ref, g_ref, o_ref):
    o_ref[...] = _rmsnorm(x_ref[...], g_ref[...], NORM_EPS)


def _final_norm(x, gain):
    s, d = x.shape
    tm = _tile(s, 512)
    return pl.pallas_call(
        _final_norm_kernel,
        grid=(s // tm,),
        in_specs=[pl.BlockSpec((tm, d), lambda i: (i, 0)),
                  pl.BlockSpec((1, d), lambda i: (0, 0))],
        out_specs=pl.BlockSpec((tm, d), lambda i: (i, 0)),
        out_shape=jax.ShapeDtypeStruct((s, d), F32),
        compiler_params=_params(1),
        name="final_norm",
    )(x, gain.reshape(1, d))


def kernel(x, norm_gains, final_norm_gain, rel_bias, attn_w_in, attn_lambda, attn_subln_gain, attn_w_out,
           pool_w_in, pool_w_group, pool_scale, pool_w_out):
    b, s, d = x.shape
    assert b == 1, "attention and pooling tiles assume one sequence"
    assert norm_gains.shape[0] == 2 and attn_w_in.shape[0] == 1 and pool_w_in.shape[0] == 1
    n_heads = rel_bias.shape[1]
    head_dim = attn_w_in.shape[2] // (8 * n_heads)
    xs = x.reshape(s, d)

    qkvz = _attn_proj(xs, norm_gains[0], attn_w_in[0].astype(BF16), head_dim ** -0.5 * LOG2E)
    og = _diff_attn(qkvz, rel_bias, attn_lambda[0], attn_subln_gain[0], n_heads, _lambda_init(0))
    x1 = _matmul_residual(og, attn_w_out[0].astype(BF16), xs, "attn_out")

    pz = _pool_proj(x1, norm_gains[1], pool_w_in[0].astype(BF16), pool_w_group.shape[1])
    mixed = _pool_group(pz, pool_w_group[0].astype(BF16), pool_scale[0])
    x2 = _matmul_residual(mixed, pool_w_out[0].astype(BF16), x1, "pool_out")

    return _final_norm(x2, final_norm_gain).reshape(b, s, d)
```

```python
import functools
import math

import jax
import jax.numpy as jnp
import numpy as np
from jax import lax
from jax.experimental import pallas as pl
from jax.experimental.pallas import tpu as pltpu

NUM_BUCKETS = 32
MAX_DISTANCE = 128
POOL_WINDOWS = (2, 4, 8, 16)
assert all(w == 2 << k for k, w in enumerate(POOL_WINDOWS)), "pooling doubles a running window sum"
NORM_EPS = 1e-6
SUBLN_EPS = 1e-5
NEG_INF = -1e30
LOG2E = math.log2(math.e)

V7X_VMEM_LIMIT_BYTES = 56 * 1024 * 1024
LANES = 128
V7X_MXU_WIDTH = 256
WEIGHT_TILE_ELEMS = 2 * 1024 * 1024
POOL_HALO = max(POOL_WINDOWS)

BF16 = jnp.bfloat16
F32 = jnp.float32


def _lambda_init(layer_idx):
    return 0.8 - 0.6 * math.exp(-0.3 * layer_idx)


def _bucket_thresholds():
    max_exact = NUM_BUCKETS // 2
    n = np.arange(0, 2 * MAX_DISTANCE, dtype=np.float64)
    large = max_exact + np.floor(
        np.log(np.maximum(n, max_exact) / max_exact) / math.log(MAX_DISTANCE / max_exact) * (NUM_BUCKETS - max_exact))
    bucket = np.where(n < max_exact, n, np.minimum(large, NUM_BUCKETS - 1)).astype(np.int64)
    return [int(np.argmax(bucket >= b)) for b in range(NUM_BUCKETS)]


def _tile(dim, target):
    t = min(dim, target)
    while dim % t:
        t -= LANES
    assert t > 0
    return t


def _params(n_axes):
    return pltpu.CompilerParams(dimension_semantics=("arbitrary",) * n_axes,
                                vmem_limit_bytes=V7X_VMEM_LIMIT_BYTES)


def _rmsnorm(x, gain, eps):
    inv = lax.rsqrt(jnp.mean(x * x, axis=-1, keepdims=True) + eps)
    return x * inv * gain


def _silu(x):
    return x * jax.nn.sigmoid(x)


def _col_chunks(n):
    step = V7X_MXU_WIDTH if n % V7X_MXU_WIDTH == 0 else n
    return [slice(c, c + step) for c in range(0, n, step)]


def _attn_proj_kernel(x_ref, g_ref, w_ref, o_ref, h_ref, raw_ref, *, n_col_tiles, q_scale, tiles_per_part):
    t = pl.program_id(0)
    n_tiles = pl.num_programs(0) - 1

    @pl.when(t == 0)
    def _():
        raw_ref[...] = jnp.zeros(raw_ref.shape, F32)

    @pl.when(jnp.minimum(t, n_tiles - 1) % n_col_tiles == 0)
    def _():
        h_ref[...] = _rmsnorm(x_ref[...], g_ref[...], NORM_EPS).astype(BF16)

    j_prev = jnp.maximum(t - 1, 0) % n_col_tiles
    scale = jnp.where(j_prev < tiles_per_part, q_scale, 1.0)
    is_gate = j_prev >= 3 * tiles_per_part
    for cols in _col_chunks(o_ref.shape[1]):
        acc = raw_ref[:, cols]
        o_ref[:, cols] = jnp.where(is_gate, _silu(acc), acc * scale).astype(o_ref.dtype)
    for cols in _col_chunks(o_ref.shape[1]):
        raw_ref[:, cols] = jnp.dot(h_ref[...], w_ref[:, cols], preferred_element_type=F32)


def _attn_proj(x, gain, w, q_scale):
    s, d = x.shape
    n = w.shape[1]
    width = n // 4
    tm, tn = _tile(s, 512), _tile(width, 1024)
    n_col_tiles = n // tn
    n_tiles = (s // tm) * n_col_tiles
    kern = functools.partial(_attn_proj_kernel, n_col_tiles=n_col_tiles, q_scale=q_scale,
                             tiles_per_part=width // tn)

    def cur(t):
        return jnp.minimum(t, n_tiles - 1)

    def prev(t):
        return jnp.maximum(t - 1, 0)

    return pl.pallas_call(
        kern,
        grid=(n_tiles + 1,),
        in_specs=[pl.BlockSpec((tm, d), lambda t: (cur(t) // n_col_tiles, 0)),
                  pl.BlockSpec((1, d), lambda t: (0, 0)),
                  pl.BlockSpec((d, tn), lambda t: (0, cur(t) % n_col_tiles))],
        out_specs=pl.BlockSpec((tm, tn), lambda t: (prev(t) // n_col_tiles, prev(t) % n_col_tiles)),
        out_shape=jax.ShapeDtypeStruct((s, n), BF16),
        scratch_shapes=[pltpu.VMEM((tm, d), BF16), pltpu.VMEM((tm, tn), F32)],
        compiler_params=_params(1),
        name="attn_proj",
    )(x, gain.reshape(1, d), w)


_NT_DIMS = (((1,), (1,)), ((), ()))


def _diff_attn_kernel(rb_ref, lam_ref, gain_ref, q_ref, k_ref, v_ref, z_ref, o_ref,
                      bias_ref, sa_ref, sb_ref, ma_ref, mb_ref, m_ref, l_ref, acc_ref,
                      *, blk, head_dim, lam_init, thresholds):
    h = pl.program_id(0)
    i = pl.program_id(1)
    lane_tiles = blk // LANES
    v_lane_tiles = acc_ref.shape[-1] // LANES

    @pl.when(i == 0)
    def _():
        row = lax.broadcasted_iota(jnp.int32, (blk, blk), 0)
        col = lax.broadcasted_iota(jnp.int32, (blk, blk), 1)
        last = rb_ref[NUM_BUCKETS - 1, h]
        for t in range(2):
            dist = row - col + t * blk
            b = jnp.zeros((blk, blk), F32)
            for bucket in range(NUM_BUCKETS - 2, -1, -1):
                val = (rb_ref[bucket, h] - last) * LOG2E
                b = jnp.where(dist < thresholds[bucket + 1], val, b)
            if t == 0:
                b = jnp.where(dist >= 0, b, NEG_INF)
            bias_ref[t] = b

    m_ref[...] = jnp.full(m_ref.shape, -3e38, F32)
    l_ref[...] = jnp.zeros(l_ref.shape, F32)
    acc_ref[...] = jnp.zeros(acc_ref.shape, F32)

    q = q_ref[...]
    qs = (q[:, :head_dim], q[:, head_dim:])

    def lane_tiles_of(s):
        return [s[:, t * LANES:(t + 1) * LANES] for t in range(lane_tiles)]

    def logits(j, bias, dst):
        s_dst, max_dst = dst
        kj = k_ref[pl.ds(pl.multiple_of(j * blk, blk), blk), :]
        for c in range(2):
            s = lax.dot_general(qs[c], kj[:, c * head_dim:(c + 1) * head_dim], _NT_DIMS,
                                preferred_element_type=F32)
            if bias is not None:
                s = s + bias
            s_dst[c] = s
            row_max = jnp.max(functools.reduce(jnp.maximum, lane_tiles_of(s)), axis=-1, keepdims=True)
            max_dst[c] = jnp.broadcast_to(row_max, (blk, LANES))

    def accumulate(j, src):
        s_src, max_src = src
        vj = v_ref[pl.ds(pl.multiple_of(j * blk, blk), blk), :]
        for c in range(2):
            tiles = lane_tiles_of(s_src[c])
            m_old = m_ref[c]
            m_new = jnp.maximum(m_old, max_src[c])
            alpha = jnp.exp2(m_old - m_new)
            ps = [jnp.exp2(t - m_new) for t in tiles]
            l_ref[c] = alpha * l_ref[c] + functools.reduce(jnp.add, ps)
            pv = jnp.dot(jnp.concatenate(ps, axis=1).astype(BF16), vj, preferred_element_type=F32)
            acc_ref[c] = jnp.concatenate([alpha] * v_lane_tiles, axis=1) * acc_ref[c] + pv
            m_ref[c] = m_new

    buf_a = (sa_ref, ma_ref)
    buf_b = (sb_ref, mb_ref)
    logits(i, bias_ref[0], buf_a)

    @pl.when(i == 0)
    def _():
        accumulate(i, buf_a)

    @pl.when(i >= 1)
    def _():
        logits(i - 1, bias_ref[1], buf_b)
        accumulate(i, buf_a)
        n_far = i - 1

        def far_pair(j):
            logits(j, None, buf_a)
            accumulate(j + 1, buf_b)
            logits(j - 1, None, buf_b)
            accumulate(j, buf_a)

        def far_quad(t, carry):
            far_pair(i - 2 - 4 * t)
            far_pair(i - 4 - 4 * t)
            return carry

        lax.fori_loop(0, n_far // 4, far_quad, 0)

        @pl.when(n_far % 4 >= 2)
        def _():
            far_pair(i - 2 - 4 * (n_far // 4))

        @pl.when(n_far % 2 == 1)
        def _():
            logits(0, None, buf_a)
            accumulate(1, buf_b)
            accumulate(0, buf_a)

        @pl.when(n_far % 2 == 0)
        def _():
            accumulate(0, buf_b)

    lp = lam_ref[...]
    lam = (jnp.exp(jnp.sum(lp[0:1] * lp[1:2], axis=-1, keepdims=True))
           - jnp.exp(jnp.sum(lp[2:3] * lp[3:4], axis=-1, keepdims=True)) + lam_init)
    inv_l = [1.0 / jnp.sum(l_ref[c], axis=-1, keepdims=True) for c in range(2)]
    o = acc_ref[0] * inv_l[0] - lam * (acc_ref[1] * inv_l[1])
    o = _rmsnorm(o, gain_ref[...], SUBLN_EPS) * (1.0 - lam_init)
    o_ref[...] = (o * z_ref[...].astype(F32)).astype(o_ref.dtype)


def _diff_attn(qkvz, rel_bias, lam_params, subln_gain, n_heads, lam_init):
    s = qkvz.shape[0]
    width = qkvz.shape[1] // 4
    v_dim = width // n_heads
    head_dim = v_dim // 2
    blk = _tile(s, 512)
    thresholds = _bucket_thresholds()
    assert thresholds[NUM_BUCKETS - 1] <= blk, "bias must be constant beyond the first sub-diagonal block"
    kern = functools.partial(_diff_attn_kernel, blk=blk, head_dim=head_dim, lam_init=lam_init,
                             thresholds=thresholds)
    return pl.pallas_call(
        kern,
        grid=(n_heads, s // blk),
        in_specs=[pl.BlockSpec(memory_space=pltpu.SMEM),
                  pl.BlockSpec(lam_params.shape, lambda h, i: (0, 0)),
                  pl.BlockSpec((1, v_dim), lambda h, i: (0, 0)),
                  pl.BlockSpec((blk, v_dim), lambda h, i: (i, h)),
                  pl.BlockSpec((s, v_dim), lambda h, i: (0, n_heads + h)),
                  pl.BlockSpec((s, v_dim), lambda h, i: (0, 2 * n_heads + h)),
                  pl.BlockSpec((blk, v_dim), lambda h, i: (i, 3 * n_heads + h))],
        out_specs=pl.BlockSpec((blk, v_dim), lambda h, i: (i, h)),
        out_shape=jax.ShapeDtypeStruct((s, width), BF16),
        scratch_shapes=[pltpu.VMEM((2, blk, blk), F32),
                        pltpu.VMEM((2, blk, blk), F32),
                        pltpu.VMEM((2, blk, blk), F32),
                        pltpu.VMEM((2, blk, LANES), F32),
                        pltpu.VMEM((2, blk, LANES), F32),
                        pltpu.VMEM((2, blk, LANES), F32),
                        pltpu.VMEM((2, blk, LANES), F32),
                        pltpu.VMEM((2, blk, v_dim), F32)],
        compiler_params=_params(2),
        name="diff_attn",
    )(rel_bias, lam_params, subln_gain.reshape(1, v_dim), qkvz, qkvz, qkvz, qkvz)


def _matmul_residual_kernel(a_ref, w_ref, r_ref, o_ref):
    for cols in _col_chunks(o_ref.shape[1]):
        o_ref[:, cols] = r_ref[:, cols] + jnp.dot(a_ref[...], w_ref[:, cols], preferred_element_type=F32)


def _matmul_residual(a, w, resid, name):
    s, k = a.shape
    n = w.shape[1]
    tm = _tile(s, 1024)
    tn = _tile(n, max(V7X_MXU_WIDTH, WEIGHT_TILE_ELEMS // k))
    return pl.pallas_call(
        _matmul_residual_kernel,
        grid=(s // tm, n // tn),
        in_specs=[pl.BlockSpec((tm, k), lambda i, j: (i, 0)),
                  pl.BlockSpec((k, tn), lambda i, j: (0, j)),
                  pl.BlockSpec((tm, tn), lambda i, j: (i, j))],
        out_specs=pl.BlockSpec((tm, tn), lambda i, j: (i, j)),
        out_shape=jax.ShapeDtypeStruct((s, n), F32),
        compiler_params=_params(2),
        name=name,
    )(a, w, resid)


def _pool_proj_kernel(x_ref, g_ref, w_ref, o_ref, h_ref, halo_ref, *, tm, n_u_tiles, tiles_per_group):
    i = pl.program_id(0)
    j = pl.program_id(1)

    @pl.when(j == 0)
    def _():
        h_ref[...] = _rmsnorm(x_ref[...], g_ref[...], NORM_EPS).astype(BF16)

    acc = jnp.dot(h_ref[...], w_ref[...], preferred_element_type=F32)

    @pl.when(j >= n_u_tiles)
    def _():
        o_ref[...] = _silu(acc).astype(o_ref.dtype)

    for g, window in enumerate(POOL_WINDOWS):
        @pl.when((j >= g * tiles_per_group) & (j < (g + 1) * tiles_per_group))
        def _(window=window):
            jj = jnp.minimum(j, n_u_tiles - 1)

            @pl.when(i == 0)
            def _():
                halo_ref[jj] = jnp.zeros(halo_ref.shape[1:], F32)

            total = jnp.concatenate([halo_ref[jj], acc], axis=0)
            shift = 1
            while shift < window:
                total = total + pltpu.roll(total, shift, axis=0)
                shift *= 2
            t = i * tm + lax.broadcasted_iota(jnp.int32, (tm, 1), 0)
            inv_cnt = 1.0 / jnp.minimum(t + 1, window).astype(F32)
            o_ref[...] = (total[POOL_HALO:] * inv_cnt - acc).astype(o_ref.dtype)
            halo_ref[jj] = acc[tm - POOL_HALO:]


def _pool_proj(x, gain, w, n_groups):
    s, d = x.shape
    n = w.shape[1]
    width = n // 2
    group_dim = width // n_groups
    tm, tn = _tile(s, 512), _tile(group_dim, 1024)
    n_u_tiles = width // tn
    kern = functools.partial(_pool_proj_kernel, tm=tm, n_u_tiles=n_u_tiles, tiles_per_group=group_dim // tn)
    return pl.pallas_call(
        kern,
        grid=(s // tm, n // tn),
        in_specs=[pl.BlockSpec((tm, d), lambda i, j: (i, 0)),
                  pl.BlockSpec((1, d), lambda i, j: (0, 0)),
                  pl.BlockSpec((d, tn), lambda i, j: (0, j))],
        out_specs=pl.BlockSpec((tm, tn), lambda i, j: (i, j)),
        out_shape=jax.ShapeDtypeStruct((s, n), BF16),
        scratch_shapes=[pltpu.VMEM((tm, d), BF16),
                        pltpu.VMEM((n_u_tiles, POOL_HALO, tn), F32)],
        compiler_params=_params(2),
        name="pool_proj",
    )(x, gain.reshape(1, d), w)


def _pool_group_kernel(p_ref, w_ref, s_ref, z_ref, o_ref):
    for cols in _col_chunks(o_ref.shape[1]):
        acc = jnp.dot(p_ref[...], w_ref[:, cols], preferred_element_type=F32)
        o_ref[:, cols] = (acc * s_ref[:, cols] * z_ref[:, cols].astype(F32)).astype(o_ref.dtype)


def _pool_group(pz, w_group, scale):
    s = pz.shape[0]
    n_groups, group_dim, _ = w_group.shape
    width = n_groups * group_dim
    tm, tn = _tile(s, 1024), _tile(group_dim, 1024)
    tpg = group_dim // tn
    return pl.pallas_call(
        _pool_group_kernel,
        grid=(s // tm, n_groups, tpg),
        in_specs=[pl.BlockSpec((tm, group_dim), lambda i, g, j: (i, g)),
                  pl.BlockSpec((None, group_dim, tn), lambda i, g, j: (g, 0, j)),
                  pl.BlockSpec((1, tn), lambda i, g, j: (0, g * tpg + j)),
                  pl.BlockSpec((tm, tn), lambda i, g, j: (i, (n_groups + g) * tpg + j))],
        out_specs=pl.BlockSpec((tm, tn), lambda i, g, j: (i, g * tpg + j)),
        out_shape=jax.ShapeDtypeStruct((s, width), BF16),
        compiler_params=_params(3),
        name="pool_group",
    )(pz, w_group, scale.reshape(1, width), pz)


def _final_norm_kernel(x_ref, g_ref, o_ref):
    o_ref[...] = _rmsnorm(x_ref[...], g_ref[...], NORM_EPS)


def _final_norm(x, gain):
    s, d = x.shape
    tm = _tile(s, 512)
    return pl.pallas_call(
        _final_norm_kernel,
        grid=(s // tm,),
        in_specs=[pl.BlockSpec((tm, d), lambda i: (i, 0)),
                  pl.BlockSpec((1, d), lambda i: (0, 0))],
        out_specs=pl.BlockSpec((tm, d), lambda i: (i, 0)),
        out_shape=jax.ShapeDtypeStruct((s, d), F32),
        compiler_params=_params(1),
        name="final_norm",
    )(x, gain.reshape(1, d))


def kernel(x, norm_gains, final_norm_gain, rel_bias, attn_w_in, attn_lambda, attn_subln_gain, attn_w_out,
           pool_w_in, pool_w_group, pool_scale, pool_w_out):
    b, s, d = x.shape
    assert b == 1, "attention and pooling tiles assume one sequence"
    assert norm_gains.shape[0] == 2 and attn_w_in.shape[0] == 1 and pool_w_in.shape[0] == 1
    n_heads = rel_bias.shape[1]
    head_dim = attn_w_in.shape[2] // (8 * n_heads)
    xs = x.reshape(s, d)

    qkvz = _attn_proj(xs, norm_gains[0], attn_w_in[0].astype(BF16), head_dim ** -0.5 * LOG2E)
    og = _diff_attn(qkvz, rel_bias, attn_lambda[0], attn_subln_gain[0], n_heads, _lambda_init(0))
    x1 = _matmul_residual(og, attn_w_out[0].astype(BF16), xs, "attn_out")

    pz = _pool_proj(x1, norm_gains[1], pool_w_in[0].astype(BF16), pool_w_group.shape[1])
    mixed = _pool_group(pz, pool_w_group[0].astype(BF16), pool_scale[0])
    x2 = _matmul_residual(mixed, pool_w_out[0].astype(BF16), x1, "pool_out")

    return _final_norm(x2, final_norm_gain).reshape(b, s, d)
```

```python
import functools
import math

import jax
import jax.numpy as jnp
import numpy as np
from jax import lax
from jax.experimental import pallas as pl
from jax.experimental.pallas import tpu as pltpu

NUM_BUCKETS = 32
MAX_DISTANCE = 128
POOL_WINDOWS = (2, 4, 8, 16)
assert all(w == 2 << k for k, w in enumerate(POOL_WINDOWS)), "pooling doubles a running window sum"
NORM_EPS = 1e-6
SUBLN_EPS = 1e-5
NEG_INF = -1e30
LOG2E = math.log2(math.e)

V7X_VMEM_LIMIT_BYTES = 56 * 1024 * 1024
LANES = 128
V7X_MXU_WIDTH = 256
WEIGHT_TILE_ELEMS = 2 * 1024 * 1024
POOL_HALO = max(POOL_WINDOWS)

BF16 = jnp.bfloat16
F32 = jnp.float32


def _lambda_init(layer_idx):
    return 0.8 - 0.6 * math.exp(-0.3 * layer_idx)


def _bucket_thresholds():
    max_exact = NUM_BUCKETS // 2
    n = np.arange(0, 2 * MAX_DISTANCE, dtype=np.float64)
    large = max_exact + np.floor(
        np.log(np.maximum(n, max_exact) / max_exact) / math.log(MAX_DISTANCE / max_exact) * (NUM_BUCKETS - max_exact))
    bucket = np.where(n < max_exact, n, np.minimum(large, NUM_BUCKETS - 1)).astype(np.int64)
    return [int(np.argmax(bucket >= b)) for b in range(NUM_BUCKETS)]


def _tile(dim, target):
    t = min(dim, target)
    while dim % t:
        t -= LANES
    assert t > 0
    return t


def _params(n_axes):
    return pltpu.CompilerParams(dimension_semantics=("arbitrary",) * n_axes,
                                vmem_limit_bytes=V7X_VMEM_LIMIT_BYTES)


def _rmsnorm(x, gain, eps):
    inv = lax.rsqrt(jnp.mean(x * x, axis=-1, keepdims=True) + eps)
    return x * inv * gain


def _silu(x):
    half = 0.5 * x
    return half + half * jnp.tanh(half)


def _col_chunks(n):
    step = V7X_MXU_WIDTH if n % V7X_MXU_WIDTH == 0 else n
    return [slice(c, c + step) for c in range(0, n, step)]


def _attn_proj_kernel(x_ref, g_ref, w_ref, o_ref, h_ref, raw_ref, *, n_col_tiles, q_scale, tiles_per_part):
    t = pl.program_id(0)
    n_tiles = pl.num_programs(0) - 1

    @pl.when(t == 0)
    def _():
        raw_ref[...] = jnp.zeros(raw_ref.shape, F32)

    @pl.when(jnp.minimum(t, n_tiles - 1) % n_col_tiles == 0)
    def _():
        h_ref[...] = _rmsnorm(x_ref[...], g_ref[...], NORM_EPS).astype(BF16)

    j_prev = jnp.maximum(t - 1, 0) % n_col_tiles
    scale = jnp.where(j_prev < tiles_per_part, q_scale, 1.0)
    is_gate = j_prev >= 3 * tiles_per_part
    for cols in _col_chunks(o_ref.shape[1]):
        acc = raw_ref[:, cols]
        o_ref[:, cols] = jnp.where(is_gate, _silu(acc), acc * scale).astype(o_ref.dtype)
    for cols in _col_chunks(o_ref.shape[1]):
        raw_ref[:, cols] = jnp.dot(h_ref[...], w_ref[:, cols], preferred_element_type=F32)


def _attn_proj(x, gain, w, q_scale):
    s, d = x.shape
    n = w.shape[1]
    width = n // 4
    tm, tn = _tile(s, 512), _tile(width, 1024)
    n_col_tiles = n // tn
    n_tiles = (s // tm) * n_col_tiles
    kern = functools.partial(_attn_proj_kernel, n_col_tiles=n_col_tiles, q_scale=q_scale,
                             tiles_per_part=width // tn)

    def cur(t):
        return jnp.minimum(t, n_tiles - 1)

    def prev(t):
        return jnp.maximum(t - 1, 0)

    return pl.pallas_call(
        kern,
        grid=(n_tiles + 1,),
        in_specs=[pl.BlockSpec((tm, d), lambda t: (cur(t) // n_col_tiles, 0)),
                  pl.BlockSpec((1, d), lambda t: (0, 0)),
                  pl.BlockSpec((d, tn), lambda t: (0, cur(t) % n_col_tiles))],
        out_specs=pl.BlockSpec((tm, tn), lambda t: (prev(t) // n_col_tiles, prev(t) % n_col_tiles)),
        out_shape=jax.ShapeDtypeStruct((s, n), BF16),
        scratch_shapes=[pltpu.VMEM((tm, d), BF16), pltpu.VMEM((tm, tn), F32)],
        compiler_params=_params(1),
        name="attn_proj",
    )(x, gain.reshape(1, d), w)


_NT_DIMS = (((1,), (1,)), ((), ()))


def _diff_attn_kernel(rb_ref, lam_ref, gain_ref, q_ref, k_ref, v_ref, z_ref, o_ref,
                      bias_ref, sa_ref, sb_ref, ma_ref, mb_ref, m_ref, l_ref, acc_ref,
                      *, blk, head_dim, lam_init, thresholds):
    h = pl.program_id(0)
    i = pl.program_id(1)
    v_lane_tiles = acc_ref.shape[-1] // LANES

    @pl.when(i == 0)
    def _():
        row = lax.broadcasted_iota(jnp.int32, bias_ref.shape, 0)
        col = lax.broadcasted_iota(jnp.int32, bias_ref.shape, 1)
        dist = row - col + blk
        last = rb_ref[NUM_BUCKETS - 1, h]
        b = jnp.zeros(bias_ref.shape, F32)
        for bucket in range(NUM_BUCKETS - 2, -1, -1):
            val = (rb_ref[bucket, h] - last) * LOG2E
            b = jnp.where(dist < thresholds[bucket + 1], val, b)
        bias_ref[...] = jnp.where(dist >= 0, b, NEG_INF)

    m_ref[...] = jnp.full(m_ref.shape, -3e38, F32)
    l_ref[...] = jnp.zeros(l_ref.shape, F32)
    acc_ref[...] = jnp.zeros(acc_ref.shape, F32)

    q = q_ref[...]
    qs = (q[:, :head_dim], q[:, head_dim:])

    def lane_tiles_of(s):
        return [s[:, t * LANES:(t + 1) * LANES] for t in range(s.shape[1] // LANES)]

    def logits(start, n_keys, bias, dst):
        s_dst, max_dst = dst
        kj = k_ref[pl.ds(start, n_keys), :]
        for c in range(2):
            s = lax.dot_general(qs[c], kj[:, c * head_dim:(c + 1) * head_dim], _NT_DIMS,
                                preferred_element_type=F32)
            if bias is not None:
                s = s + bias
            s_dst[c, :, :n_keys] = s
            row_max = jnp.max(functools.reduce(jnp.maximum, lane_tiles_of(s)), axis=-1, keepdims=True)
            max_dst[c] = jnp.broadcast_to(row_max, (blk, LANES))

    def accumulate(start, n_keys, src):
        s_src, max_src = src
        vj = v_ref[pl.ds(start, n_keys), :]
        for c in range(2):
            tiles = lane_tiles_of(s_src[c, :, :n_keys])
            m_old = m_ref[c]
            m_new = jnp.maximum(m_old, max_src[c])
            alpha = jnp.exp2(m_old - m_new)
            ps = [jnp.exp2(t - m_new) for t in tiles]
            l_ref[c] = alpha * l_ref[c] + functools.reduce(jnp.add, ps)
            pv = jnp.dot(jnp.concatenate(ps, axis=1).astype(BF16), vj, preferred_element_type=F32)
            acc_ref[c] = jnp.concatenate([alpha] * v_lane_tiles, axis=1) * acc_ref[c] + pv
            m_ref[c] = m_new

    buf_a = (sa_ref, ma_ref)
    buf_b = (sb_ref, mb_ref)
    span = 2 * blk

    def span_start(u):
        return pl.multiple_of((i - 3 - 2 * u) * blk, blk)

    @pl.when(i == 0)
    def _():
        logits(0, blk, bias_ref[:, blk:], buf_a)
        accumulate(0, blk, buf_a)

    @pl.when(i >= 1)
    def _():
        logits(span_start(-1), span, bias_ref[...], buf_a)
        n_far = i - 1
        n_spans = n_far // 2
        odd_span = n_spans % 2 == 1
        odd_block = n_far % 2 == 1

        def two_spans(t, carry):
            logits(span_start(2 * t), span, None, buf_b)
            accumulate(span_start(2 * t - 1), span, buf_a)
            logits(span_start(2 * t + 1), span, None, buf_a)
            accumulate(span_start(2 * t), span, buf_b)
            return carry

        lax.fori_loop(0, n_spans // 2, two_spans, 0)
        pending = 2 * (n_spans // 2) - 1

        @pl.when(jnp.logical_not(odd_span) & jnp.logical_not(odd_block))
        def _():
            accumulate(span_start(pending), span, buf_a)

        @pl.when(jnp.logical_not(odd_span) & odd_block)
        def _():
            logits(0, blk, None, buf_b)
            accumulate(span_start(pending), span, buf_a)
            accumulate(0, blk, buf_b)

        @pl.when(odd_span & jnp.logical_not(odd_block))
        def _():
            logits(span_start(pending + 1), span, None, buf_b)
            accumulate(span_start(pending), span, buf_a)
            accumulate(span_start(pending + 1), span, buf_b)

        @pl.when(odd_span & odd_block)
        def _():
            logits(span_start(pending + 1), span, None, buf_b)
            accumulate(span_start(pending), span, buf_a)
            logits(0, blk, None, buf_a)
            accumulate(span_start(pending + 1), span, buf_b)
            accumulate(0, blk, buf_a)

    lp = lam_ref[...]
    lam = (jnp.exp(jnp.sum(lp[0:1] * lp[1:2], axis=-1, keepdims=True))
           - jnp.exp(jnp.sum(lp[2:3] * lp[3:4], axis=-1, keepdims=True)) + lam_init)
    inv_l = [1.0 / jnp.sum(l_ref[c], axis=-1, keepdims=True) for c in range(2)]
    o = acc_ref[0] * inv_l[0] - lam * (acc_ref[1] * inv_l[1])
    o = _rmsnorm(o, gain_ref[...], SUBLN_EPS) * (1.0 - lam_init)
    o_ref[...] = (o * z_ref[...].astype(F32)).astype(o_ref.dtype)


def _diff_attn(qkvz, rel_bias, lam_params, subln_gain, n_heads, lam_init):
    s = qkvz.shape[0]
    width = qkvz.shape[1] // 4
    v_dim = width // n_heads
    head_dim = v_dim // 2
    blk = _tile(s, 512)
    thresholds = _bucket_thresholds()
    assert thresholds[NUM_BUCKETS - 1] <= blk, "bias must be constant beyond the first sub-diagonal block"
    kern = functools.partial(_diff_attn_kernel, blk=blk, head_dim=head_dim, lam_init=lam_init,
                             thresholds=thresholds)
    return pl.pallas_call(
        kern,
        grid=(n_heads, s // blk),
        in_specs=[pl.BlockSpec(memory_space=pltpu.SMEM),
                  pl.BlockSpec(lam_params.shape, lambda h, i: (0, 0)),
                  pl.BlockSpec((1, v_dim), lambda h, i: (0, 0)),
                  pl.BlockSpec((blk, v_dim), lambda h, i: (i, h)),
                  pl.BlockSpec((s, v_dim), lambda h, i: (0, n_heads + h)),
                  pl.BlockSpec((s, v_dim), lambda h, i: (0, 2 * n_heads + h)),
                  pl.BlockSpec((blk, v_dim), lambda h, i: (i, 3 * n_heads + h))],
        out_specs=pl.BlockSpec((blk, v_dim), lambda h, i: (i, h)),
        out_shape=jax.ShapeDtypeStruct((s, width), BF16),
        scratch_shapes=[pltpu.VMEM((blk, 2 * blk), F32),
                        pltpu.VMEM((2, blk, 2 * blk), F32),
                        pltpu.VMEM((2, blk, 2 * blk), F32),
                        pltpu.VMEM((2, blk, LANES), F32),
                        pltpu.VMEM((2, blk, LANES), F32),
                        pltpu.VMEM((2, blk, LANES), F32),
                        pltpu.VMEM((2, blk, LANES), F32),
                        pltpu.VMEM((2, blk, v_dim), F32)],
        compiler_params=_params(2),
        name="diff_attn",
    )(rel_bias, lam_params, subln_gain.reshape(1, v_dim), qkvz, qkvz, qkvz, qkvz)


def _matmul_residual_kernel(a_ref, w_ref, r_ref, o_ref):
    for cols in _col_chunks(o_ref.shape[1]):
        o_ref[:, cols] = r_ref[:, cols] + jnp.dot(a_ref[...], w_ref[:, cols], preferred_element_type=F32)


def _matmul_residual(a, w, resid, name):
    s, k = a.shape
    n = w.shape[1]
    tm = _tile(s, 1024)
    tn = _tile(n, max(V7X_MXU_WIDTH, WEIGHT_TILE_ELEMS // k))
    return pl.pallas_call(
        _matmul_residual_kernel,
        grid=(s // tm, n // tn),
        in_specs=[pl.BlockSpec((tm, k), lambda i, j: (i, 0)),
                  pl.BlockSpec((k, tn), lambda i, j: (0, j)),
                  pl.BlockSpec((tm, tn), lambda i, j: (i, j))],
        out_specs=pl.BlockSpec((tm, tn), lambda i, j: (i, j)),
        out_shape=jax.ShapeDtypeStruct((s, n), F32),
        compiler_params=_params(2),
        name=name,
    )(a, w, resid)


def _pool_proj_kernel(x_ref, g_ref, w_ref, o_ref, h_ref, halo_ref, *, tm, n_u_tiles, tiles_per_group):
    i = pl.program_id(0)
    j = pl.program_id(1)

    @pl.when(j == 0)
    def _():
        h_ref[...] = _rmsnorm(x_ref[...], g_ref[...], NORM_EPS).astype(BF16)

    acc = jnp.dot(h_ref[...], w_ref[...], preferred_element_type=F32)

    @pl.when(j >= n_u_tiles)
    def _():
        o_ref[...] = _silu(acc).astype(o_ref.dtype)

    for g, window in enumerate(POOL_WINDOWS):
        @pl.when((j >= g * tiles_per_group) & (j < (g + 1) * tiles_per_group))
        def _(window=window):
            jj = jnp.minimum(j, n_u_tiles - 1)

            @pl.when(i == 0)
            def _():
                halo_ref[jj] = jnp.zeros(halo_ref.shape[1:], F32)

            total = jnp.concatenate([halo_ref[jj], acc], axis=0)
            shift = 1
            while shift < window:
                total = total + pltpu.roll(total, shift, axis=0)
                shift *= 2
            t = i * tm + lax.broadcasted_iota(jnp.int32, (tm, 1), 0)
            inv_cnt = 1.0 / jnp.minimum(t + 1, window).astype(F32)
            o_ref[...] = (total[POOL_HALO:] * inv_cnt - acc).astype(o_ref.dtype)
            halo_ref[jj] = acc[tm - POOL_HALO:]


def _pool_proj(x, gain, w, n_groups):
    s, d = x.shape
    n = w.shape[1]
    width = n // 2
    group_dim = width // n_groups
    tm, tn = _tile(s, 512), _tile(group_dim, 1024)
    n_u_tiles = width // tn
    kern = functools.partial(_pool_proj_kernel, tm=tm, n_u_tiles=n_u_tiles, tiles_per_group=group_dim // tn)
    return pl.pallas_call(
        kern,
        grid=(s // tm, n // tn),
        in_specs=[pl.BlockSpec((tm, d), lambda i, j: (i, 0)),
                  pl.BlockSpec((1, d), lambda i, j: (0, 0)),
                  pl.BlockSpec((d, tn), lambda i, j: (0, j))],
        out_specs=pl.BlockSpec((tm, tn), lambda i, j: (i, j)),
        out_shape=jax.ShapeDtypeStruct((s, n), BF16),
        scratch_shapes=[pltpu.VMEM((tm, d), BF16),
                        pltpu.VMEM((n_u_tiles, POOL_HALO, tn), F32)],
        compiler_params=_params(2),
        name="pool_proj",
    )(x, gain.reshape(1, d), w)


def _pool_group_kernel(p_ref, w_ref, s_ref, z_ref, o_ref):
    for cols in _col_chunks(o_ref.shape[1]):
        acc = jnp.dot(p_ref[...], w_ref[:, cols], preferred_element_type=F32)
        o_ref[:, cols] = (acc * s_ref[:, cols] * z_ref[:, cols].astype(F32)).astype(o_ref.dtype)


def _pool_group(pz, w_group, scale):
    s = pz.shape[0]
    n_groups, group_dim, _ = w_group.shape
    width = n_groups * group_dim
    tm, tn = _tile(s, 1024), _tile(group_dim, 1024)
    tpg = group_dim // tn
    return pl.pallas_call(
        _pool_group_kernel,
        grid=(s // tm, n_groups, tpg),
        in_specs=[pl.BlockSpec((tm, group_dim), lambda i, g, j: (i, g)),
                  pl.BlockSpec((None, group_dim, tn), lambda i, g, j: (g, 0, j)),
                  pl.BlockSpec((1, tn), lambda i, g, j: (0, g * tpg + j)),
                  pl.BlockSpec((tm, tn), lambda i, g, j: (i, (n_groups + g) * tpg + j))],
        out_specs=pl.BlockSpec((tm, tn), lambda i, g, j: (i, g * tpg + j)),
        out_shape=jax.ShapeDtypeStruct((s, width), BF16),
        compiler_params=_params(3),
        name="pool_group",
    )(pz, w_group, scale.reshape(1, width), pz)


def _final_norm_kernel(x_ref, g_ref, o_ref):
    o_ref[...] = _rmsnorm(x_ref[...], g_ref[...], NORM_EPS)


def _final_norm(x, gain):
    s, d = x.shape
    tm = _tile(s, 512)
    return pl.pallas_call(
        _final_norm_kernel,
        grid=(s // tm,),
        in_specs=[pl.BlockSpec((tm, d), lambda i: (i, 0)),
                  pl.BlockSpec((1, d), lambda i: (0, 0))],
        out_specs=pl.BlockSpec((tm, d), lambda i: (i, 0)),
        out_shape=jax.ShapeDtypeStruct((s, d), F32),
        compiler_params=_params(1),
        name="final_norm",
    )(x, gain.reshape(1, d))


def kernel(x, norm_gains, final_norm_gain, rel_bias, attn_w_in, attn_lambda, attn_subln_gain, attn_w_out,
           pool_w_in, pool_w_group, pool_scale, pool_w_out):
    b, s, d = x.shape
    assert b == 1, "attention and pooling tiles assume one sequence"
    assert norm_gains.shape[0] == 2 and attn_w_in.shape[0] == 1 and pool_w_in.shape[0] == 1
    n_heads = rel_bias.shape[1]
    head_dim = attn_w_in.shape[2] // (8 * n_heads)
    xs = x.reshape(s, d)

    qkvz = _attn_proj(xs, norm_gains[0], attn_w_in[0].astype(BF16), head_dim ** -0.5 * LOG2E)
    og = _diff_attn(qkvz, rel_bias, attn_lambda[0], attn_subln_gain[0], n_heads, _lambda_init(0))
    x1 = _matmul_residual(og, attn_w_out[0].astype(BF16), xs, "attn_out")

    pz = _pool_proj(x1, norm_gains[1], pool_w_in[0].astype(BF16), pool_w_group.shape[1])
    mixed = _pool_group(pz, pool_w_group[0].astype(BF16), pool_scale[0])
    x2 = _matmul_residual(mixed, pool_w_out[0].astype(BF16), x1, "pool_out")

    return _final_norm(x2, final_norm_gain).reshape(b, s, d)
```

```python
import functools
import math

import jax
import jax.numpy as jnp
import numpy as np
from jax import lax
from jax.experimental import pallas as pl
from jax.experimental.pallas import tpu as pltpu

NUM_BUCKETS = 32
MAX_DISTANCE = 128
POOL_WINDOWS = (2, 4, 8, 16)
assert all(w == 2 << k for k, w in enumerate(POOL_WINDOWS)), "pooling doubles a running window sum"
NORM_EPS = 1e-6
SUBLN_EPS = 1e-5
NEG_INF = -1e30
LOG2E = math.log2(math.e)

V7X_VMEM_LIMIT_BYTES = 56 * 1024 * 1024
LANES = 128
V7X_MXU_WIDTH = 256
WEIGHT_TILE_ELEMS = 2 * 1024 * 1024
POOL_HALO = max(POOL_WINDOWS)

BF16 = jnp.bfloat16
F32 = jnp.float32


def _lambda_init(layer_idx):
    return 0.8 - 0.6 * math.exp(-0.3 * layer_idx)


def _bucket_thresholds():
    max_exact = NUM_BUCKETS // 2
    n = np.arange(0, 2 * MAX_DISTANCE, dtype=np.float64)
    large = max_exact + np.floor(
        np.log(np.maximum(n, max_exact) / max_exact) / math.log(MAX_DISTANCE / max_exact) * (NUM_BUCKETS - max_exact))
    bucket = np.where(n < max_exact, n, np.minimum(large, NUM_BUCKETS - 1)).astype(np.int64)
    return [int(np.argmax(bucket >= b)) for b in range(NUM_BUCKETS)]


def _tile(dim, target):
    t = min(dim, target)
    while dim % t:
        t -= LANES
    assert t > 0
    return t


def _params(n_axes):
    return pltpu.CompilerParams(dimension_semantics=("arbitrary",) * n_axes,
                                vmem_limit_bytes=V7X_VMEM_LIMIT_BYTES)


def _rmsnorm(x, gain, eps):
    inv = lax.rsqrt(jnp.mean(x * x, axis=-1, keepdims=True) + eps)
    return x * inv * gain


def _silu(x):
    half = 0.5 * x
    return half + half * jnp.tanh(half)


def _col_chunks(n):
    step = V7X_MXU_WIDTH if n % V7X_MXU_WIDTH == 0 else n
    return [slice(c, c + step) for c in range(0, n, step)]


def _attn_proj_kernel(x_ref, g_ref, w_ref, o_ref, h_ref, raw_ref, *, n_col_tiles, q_scale, tiles_per_part):
    t = pl.program_id(0)
    n_tiles = pl.num_programs(0) - 1

    @pl.when(t == 0)
    def _():
        raw_ref[...] = jnp.zeros(raw_ref.shape, F32)

    @pl.when(jnp.minimum(t, n_tiles - 1) % n_col_tiles == 0)
    def _():
        h_ref[...] = _rmsnorm(x_ref[...], g_ref[...], NORM_EPS).astype(BF16)

    j_prev = jnp.maximum(t - 1, 0) % n_col_tiles
    scale = jnp.where(j_prev < tiles_per_part, q_scale, 1.0)
    is_gate = j_prev >= 3 * tiles_per_part
    for cols in _col_chunks(o_ref.shape[1]):
        acc = raw_ref[:, cols]
        o_ref[:, cols] = jnp.where(is_gate, _silu(acc), acc * scale).astype(o_ref.dtype)
    for cols in _col_chunks(o_ref.shape[1]):
        raw_ref[:, cols] = jnp.dot(h_ref[...], w_ref[:, cols], preferred_element_type=F32)


def _attn_proj(x, gain, w, q_scale):
    s, d = x.shape
    n = w.shape[1]
    width = n // 4
    tm, tn = _tile(s, 512), _tile(width, 1024)
    n_col_tiles = n // tn
    n_tiles = (s // tm) * n_col_tiles
    kern = functools.partial(_attn_proj_kernel, n_col_tiles=n_col_tiles, q_scale=q_scale,
                             tiles_per_part=width // tn)

    def cur(t):
        return jnp.minimum(t, n_tiles - 1)

    def prev(t):
        return jnp.maximum(t - 1, 0)

    return pl.pallas_call(
        kern,
        grid=(n_tiles + 1,),
        in_specs=[pl.BlockSpec((tm, d), lambda t: (cur(t) // n_col_tiles, 0)),
                  pl.BlockSpec((1, d), lambda t: (0, 0)),
                  pl.BlockSpec((d, tn), lambda t: (0, cur(t) % n_col_tiles))],
        out_specs=pl.BlockSpec((tm, tn), lambda t: (prev(t) // n_col_tiles, prev(t) % n_col_tiles)),
        out_shape=jax.ShapeDtypeStruct((s, n), BF16),
        scratch_shapes=[pltpu.VMEM((tm, d), BF16), pltpu.VMEM((tm, tn), F32)],
        compiler_params=_params(1),
        name="attn_proj",
    )(x, gain.reshape(1, d), w)


_NT_DIMS = (((1,), (1,)), ((), ()))


def _diff_attn_kernel(rb_ref, lam_ref, gain_ref, q_ref, k_ref, v_ref, z_ref, o_ref,
                      bias_ref, sa_ref, sb_ref, ma_ref, mb_ref, m_ref, l_ref, acc_ref,
                      *, blk, head_dim, lam_init, thresholds):
    h = pl.program_id(0)
    i = pl.program_id(1)
    v_lane_tiles = acc_ref.shape[-1] // LANES

    @pl.when(i == 0)
    def _():
        row = lax.broadcasted_iota(jnp.int32, bias_ref.shape, 0)
        col = lax.broadcasted_iota(jnp.int32, bias_ref.shape, 1)
        dist = row - col + blk
        last = rb_ref[NUM_BUCKETS - 1, h]
        b = jnp.zeros(bias_ref.shape, F32)
        for bucket in range(NUM_BUCKETS - 2, -1, -1):
            val = (rb_ref[bucket, h] - last) * LOG2E
            b = jnp.where(dist < thresholds[bucket + 1], val, b)
        bias_ref[...] = jnp.where(dist >= 0, b, NEG_INF)

    m_ref[...] = jnp.full(m_ref.shape, -3e38, F32)
    l_ref[...] = jnp.zeros(l_ref.shape, F32)
    acc_ref[...] = jnp.zeros(acc_ref.shape, F32)

    q = q_ref[...]
    qs = (q[:, :head_dim], q[:, head_dim:])

    def lane_tiles_of(s):
        return [s[:, t * LANES:(t + 1) * LANES] for t in range(s.shape[1] // LANES)]

    def logits(start, n_keys, bias, dst):
        s_dst, max_dst = dst
        kj = k_ref[pl.ds(start, n_keys), :]
        for c in range(2):
            s = lax.dot_general(qs[c], kj[:, c * head_dim:(c + 1) * head_dim], _NT_DIMS,
                                preferred_element_type=F32)
            if bias is not None:
                s = s + bias
            s_dst[c, :, :n_keys] = s
            row_max = jnp.max(functools.reduce(jnp.maximum, lane_tiles_of(s)), axis=-1, keepdims=True)
            max_dst[c] = jnp.broadcast_to(row_max, (blk, LANES))

    def accumulate(start, n_keys, src):
        s_src, max_src = src
        vj = v_ref[pl.ds(start, n_keys), :]
        for c in range(2):
            tiles = lane_tiles_of(s_src[c, :, :n_keys])
            m_old = m_ref[c]
            m_new = jnp.maximum(m_old, max_src[c])
            alpha = jnp.exp2(m_old - m_new)
            ps = [jnp.exp2(t - m_new) for t in tiles]
            l_ref[c] = alpha * l_ref[c] + functools.reduce(jnp.add, ps)
            pv = jnp.dot(jnp.concatenate(ps, axis=1).astype(BF16), vj, preferred_element_type=F32)
            acc_ref[c] = jnp.concatenate([alpha] * v_lane_tiles, axis=1) * acc_ref[c] + pv
            m_ref[c] = m_new

    buf_a = (sa_ref, ma_ref)
    buf_b = (sb_ref, mb_ref)
    span = 2 * blk

    def span_start(u):
        return pl.multiple_of((i - 3 - 2 * u) * blk, blk)

    @pl.when(i == 0)
    def _():
        logits(0, blk, bias_ref[:, blk:], buf_a)
        accumulate(0, blk, buf_a)

    @pl.when(i >= 1)
    def _():
        logits(span_start(-1), span, bias_ref[...], buf_a)
        n_far = i - 1
        n_spans = n_far // 2
        odd_span = n_spans % 2 == 1
        odd_block = n_far % 2 == 1

        def two_spans(t):
            logits(span_start(2 * t), span, None, buf_b)
            accumulate(span_start(2 * t - 1), span, buf_a)
            logits(span_start(2 * t + 1), span, None, buf_a)
            accumulate(span_start(2 * t), span, buf_b)

        def four_spans(t, carry):
            two_spans(2 * t)
            two_spans(2 * t + 1)
            return carry

        lax.fori_loop(0, n_spans // 4, four_spans, 0)

        @pl.when(n_spans % 4 >= 2)
        def _():
            two_spans(2 * (n_spans // 4))

        pending = 2 * (n_spans // 2) - 1

        @pl.when(jnp.logical_not(odd_span) & jnp.logical_not(odd_block))
        def _():
            accumulate(span_start(pending), span, buf_a)

        @pl.when(jnp.logical_not(odd_span) & odd_block)
        def _():
            logits(0, blk, None, buf_b)
            accumulate(span_start(pending), span, buf_a)
            accumulate(0, blk, buf_b)

        @pl.when(odd_span & jnp.logical_not(odd_block))
        def _():
            logits(span_start(pending + 1), span, None, buf_b)
            accumulate(span_start(pending), span, buf_a)
            accumulate(span_start(pending + 1), span, buf_b)

        @pl.when(odd_span & odd_block)
        def _():
            logits(span_start(pending + 1), span, None, buf_b)
            accumulate(span_start(pending), span, buf_a)
            logits(0, blk, None, buf_a)
            accumulate(span_start(pending + 1), span, buf_b)
            accumulate(0, blk, buf_a)

    lp = lam_ref[...]
    lam = (jnp.exp(jnp.sum(lp[0:1] * lp[1:2], axis=-1, keepdims=True))
           - jnp.exp(jnp.sum(lp[2:3] * lp[3:4], axis=-1, keepdims=True)) + lam_init)
    inv_l = [1.0 / jnp.sum(l_ref[c], axis=-1, keepdims=True) for c in range(2)]
    o = acc_ref[0] * inv_l[0] - lam * (acc_ref[1] * inv_l[1])
    o = _rmsnorm(o, gain_ref[...], SUBLN_EPS) * (1.0 - lam_init)
    o_ref[...] = (o * z_ref[...].astype(F32)).astype(o_ref.dtype)


def _diff_attn(qkvz, rel_bias, lam_params, subln_gain, n_heads, lam_init):
    s = qkvz.shape[0]
    width = qkvz.shape[1] // 4
    v_dim = width // n_heads
    head_dim = v_dim // 2
    blk = _tile(s, 512)
    thresholds = _bucket_thresholds()
    assert thresholds[NUM_BUCKETS - 1] <= blk, "bias must be constant beyond the first sub-diagonal block"
    kern = functools.partial(_diff_attn_kernel, blk=blk, head_dim=head_dim, lam_init=lam_init,
                             thresholds=thresholds)
    return pl.pallas_call(
        kern,
        grid=(n_heads, s // blk),
        in_specs=[pl.BlockSpec(memory_space=pltpu.SMEM),
                  pl.BlockSpec(lam_params.shape, lambda h, i: (0, 0)),
                  pl.BlockSpec((1, v_dim), lambda h, i: (0, 0)),
                  pl.BlockSpec((blk, v_dim), lambda h, i: (i, h)),
                  pl.BlockSpec((s, v_dim), lambda h, i: (0, n_heads + h)),
                  pl.BlockSpec((s, v_dim), lambda h, i: (0, 2 * n_heads + h)),
                  pl.BlockSpec((blk, v_dim), lambda h, i: (i, 3 * n_heads + h))],
        out_specs=pl.BlockSpec((blk, v_dim), lambda h, i: (i, h)),
        out_shape=jax.ShapeDtypeStruct((s, width), BF16),
        scratch_shapes=[pltpu.VMEM((blk, 2 * blk), F32),
                        pltpu.VMEM((2, blk, 2 * blk), F32),
                        pltpu.VMEM((2, blk, 2 * blk), F32),
                        pltpu.VMEM((2, blk, LANES), F32),
                        pltpu.VMEM((2, blk, LANES), F32),
                        pltpu.VMEM((2, blk, LANES), F32),
                        pltpu.VMEM((2, blk, LANES), F32),
                        pltpu.VMEM((2, blk, v_dim), F32)],
        compiler_params=_params(2),
        name="diff_attn",
    )(rel_bias, lam_params, subln_gain.reshape(1, v_dim), qkvz, qkvz, qkvz, qkvz)


def _matmul_residual_kernel(a_ref, w_ref, r_ref, o_ref):
    for cols in _col_chunks(o_ref.shape[1]):
        o_ref[:, cols] = r_ref[:, cols] + jnp.dot(a_ref[...], w_ref[:, cols], preferred_element_type=F32)


def _matmul_residual(a, w, resid, name):
    s, k = a.shape
    n = w.shape[1]
    tm = _tile(s, 1024)
    tn = _tile(n, max(V7X_MXU_WIDTH, WEIGHT_TILE_ELEMS // k))
    return pl.pallas_call(
        _matmul_residual_kernel,
        grid=(s // tm, n // tn),
        in_specs=[pl.BlockSpec((tm, k), lambda i, j: (i, 0)),
                  pl.BlockSpec((k, tn), lambda i, j: (0, j)),
                  pl.BlockSpec((tm, tn), lambda i, j: (i, j))],
        out_specs=pl.BlockSpec((tm, tn), lambda i, j: (i, j)),
        out_shape=jax.ShapeDtypeStruct((s, n), F32),
        compiler_params=_params(2),
        name=name,
    )(a, w, resid)


def _pool_proj_kernel(x_ref, g_ref, w_ref, o_ref, h_ref, halo_ref, *, tm, n_u_tiles, tiles_per_group):
    i = pl.program_id(0)
    j = pl.program_id(1)

    @pl.when(j == 0)
    def _():
        h_ref[...] = _rmsnorm(x_ref[...], g_ref[...], NORM_EPS).astype(BF16)

    acc = jnp.dot(h_ref[...], w_ref[...], preferred_element_type=F32)

    @pl.when(j >= n_u_tiles)
    def _():
        o_ref[...] = _silu(acc).astype(o_ref.dtype)

    for g, window in enumerate(POOL_WINDOWS):
        @pl.when((j >= g * tiles_per_group) & (j < (g + 1) * tiles_per_group))
        def _(window=window):
            jj = jnp.minimum(j, n_u_tiles - 1)

            @pl.when(i == 0)
            def _():
                halo_ref[jj] = jnp.zeros(halo_ref.shape[1:], F32)

            total = jnp.concatenate([halo_ref[jj], acc], axis=0)
            shift = 1
            while shift < window:
                total = total + pltpu.roll(total, shift, axis=0)
                shift *= 2
            t = i * tm + lax.broadcasted_iota(jnp.int32, (tm, 1), 0)
            inv_cnt = 1.0 / jnp.minimum(t + 1, window).astype(F32)
            o_ref[...] = (total[POOL_HALO:] * inv_cnt - acc).astype(o_ref.dtype)
            halo_ref[jj] = acc[tm - POOL_HALO:]


def _pool_proj(x, gain, w, n_groups):
    s, d = x.shape
    n = w.shape[1]
    width = n // 2
    group_dim = width // n_groups
    tm, tn = _tile(s, 512), _tile(group_dim, 1024)
    n_u_tiles = width // tn
    kern = functools.partial(_pool_proj_kernel, tm=tm, n_u_tiles=n_u_tiles, tiles_per_group=group_dim // tn)
    return pl.pallas_call(
        kern,
        grid=(s // tm, n // tn),
        in_specs=[pl.BlockSpec((tm, d), lambda i, j: (i, 0)),
                  pl.BlockSpec((1, d), lambda i, j: (0, 0)),
                  pl.BlockSpec((d, tn), lambda i, j: (0, j))],
        out_specs=pl.BlockSpec((tm, tn), lambda i, j: (i, j)),
        out_shape=jax.ShapeDtypeStruct((s, n), BF16),
        scratch_shapes=[pltpu.VMEM((tm, d), BF16),
                        pltpu.VMEM((n_u_tiles, POOL_HALO, tn), F32)],
        compiler_params=_params(2),
        name="pool_proj",
    )(x, gain.reshape(1, d), w)


def _pool_group_kernel(p_ref, w_ref, s_ref, z_ref, o_ref):
    for cols in _col_chunks(o_ref.shape[1]):
        acc = jnp.dot(p_ref[...], w_ref[:, cols], preferred_element_type=F32)
        o_ref[:, cols] = (acc * s_ref[:, cols] * z_ref[:, cols].astype(F32)).astype(o_ref.dtype)


def _pool_group(pz, w_group, scale):
    s = pz.shape[0]
    n_groups, group_dim, _ = w_group.shape
    width = n_groups * group_dim
    tm, tn = _tile(s, 1024), _tile(group_dim, 2048)
    tpg = group_dim // tn
    return pl.pallas_call(
        _pool_group_kernel,
        grid=(s // tm, n_groups, tpg),
        in_specs=[pl.BlockSpec((tm, group_dim), lambda i, g, j: (i, g)),
                  pl.BlockSpec((None, group_dim, tn), lambda i, g, j: (g, 0, j)),
                  pl.BlockSpec((1, tn), lambda i, g, j: (0, g * tpg + j)),
                  pl.BlockSpec((tm, tn), lambda i, g, j: (i, (n_groups + g) * tpg + j))],
        out_specs=pl.BlockSpec((tm, tn), lambda i, g, j: (i, g * tpg + j)),
        out_shape=jax.ShapeDtypeStruct((s, width), BF16),
        compiler_params=_params(3),
        name="pool_group",
    )(pz, w_group, scale.reshape(1, width), pz)


def _final_norm_kernel(x_ref, g_ref, o_ref):
    o_ref[...] = _rmsnorm(x_ref[...], g_ref[...], NORM_EPS)


def _final_norm(x, gain):
    s, d = x.shape
    tm = _tile(s, 512)
    return pl.pallas_call(
        _final_norm_kernel,
        grid=(s // tm,),
        in_specs=[pl.BlockSpec((tm, d), lambda i: (i, 0)),
                  pl.BlockSpec((1, d), lambda i: (0, 0))],
        out_specs=pl.BlockSpec((tm, d), lambda i: (i, 0)),
        out_shape=jax.ShapeDtypeStruct((s, d), F32),
        compiler_params=_params(1),
        name="final_norm",
    )(x, gain.reshape(1, d))


def kernel(x, norm_gains, final_norm_gain, rel_bias, attn_w_in, attn_lambda, attn_subln_gain, attn_w_out,
           pool_w_in, pool_w_group, pool_scale, pool_w_out):
    b, s, d = x.shape
    assert b == 1, "attention and pooling tiles assume one sequence"
    assert norm_gains.shape[0] == 2 and attn_w_in.shape[0] == 1 and pool_w_in.shape[0] == 1
    n_heads = rel_bias.shape[1]
    head_dim = attn_w_in.shape[2] // (8 * n_heads)
    xs = x.reshape(s, d)

    qkvz = _attn_proj(xs, norm_gains[0], attn_w_in[0].astype(BF16), head_dim ** -0.5 * LOG2E)
    og = _diff_attn(qkvz, rel_bias, attn_lambda[0], attn_subln_gain[0], n_heads, _lambda_init(0))
    x1 = _matmul_residual(og, attn_w_out[0].astype(BF16), xs, "attn_out")

    pz = _pool_proj(x1, norm_gains[1], pool_w_in[0].astype(BF16), pool_w_group.shape[1])
    mixed = _pool_group(pz, pool_w_group[0].astype(BF16), pool_scale[0])
    x2 = _matmul_residual(mixed, pool_w_out[0].astype(BF16), x1, "pool_out")

    return _final_norm(x2, final_norm_gain).reshape(b, s, d)
```

```python
import functools
import math

import jax
import jax.numpy as jnp
import numpy as np
from jax import lax
from jax.experimental import pallas as pl
from jax.experimental.pallas import tpu as pltpu

NUM_BUCKETS = 32
MAX_DISTANCE = 128
POOL_WINDOWS = (2, 4, 8, 16)
assert all(w == 2 << k for k, w in enumerate(POOL_WINDOWS)), "pooling doubles a running window sum"
NORM_EPS = 1e-6
SUBLN_EPS = 1e-5
NEG_INF = -1e30
LOG2E = math.log2(math.e)

V7X_VMEM_LIMIT_BYTES = 56 * 1024 * 1024
LANES = 128
V7X_MXU_WIDTH = 256
WEIGHT_TILE_ELEMS = 2 * 1024 * 1024
POOL_HALO = max(POOL_WINDOWS)

BF16 = jnp.bfloat16
F32 = jnp.float32


def _lambda_init(layer_idx):
    return 0.8 - 0.6 * math.exp(-0.3 * layer_idx)


def _bucket_thresholds():
    max_exact = NUM_BUCKETS // 2
    n = np.arange(0, 2 * MAX_DISTANCE, dtype=np.float64)
    large = max_exact + np.floor(
        np.log(np.maximum(n, max_exact) / max_exact) / math.log(MAX_DISTANCE / max_exact) * (NUM_BUCKETS - max_exact))
    bucket = np.where(n < max_exact, n, np.minimum(large, NUM_BUCKETS - 1)).astype(np.int64)
    return [int(np.argmax(bucket >= b)) for b in range(NUM_BUCKETS)]


def _tile(dim, target):
    t = min(dim, target)
    while dim % t:
        t -= LANES
    assert t > 0
    return t


def _params(n_axes):
    return pltpu.CompilerParams(dimension_semantics=("arbitrary",) * n_axes,
                                vmem_limit_bytes=V7X_VMEM_LIMIT_BYTES)


def _rmsnorm(x, gain, eps):
    inv = lax.rsqrt(jnp.mean(x * x, axis=-1, keepdims=True) + eps)
    return x * inv * gain


def _silu(x):
    half = 0.5 * x
    return half + half * jnp.tanh(half)


def _col_chunks(n):
    step = V7X_MXU_WIDTH if n % V7X_MXU_WIDTH == 0 else n
    return [slice(c, c + step) for c in range(0, n, step)]


def _attn_proj_kernel(x_ref, g_ref, w_ref, o_ref, h_ref, raw_ref, *, n_col_tiles, q_scale, tiles_per_part):
    t = pl.program_id(0)
    n_tiles = pl.num_programs(0) - 1

    @pl.when(t == 0)
    def _():
        raw_ref[...] = jnp.zeros(raw_ref.shape, F32)

    @pl.when(jnp.minimum(t, n_tiles - 1) % n_col_tiles == 0)
    def _():
        h_ref[...] = _rmsnorm(x_ref[...], g_ref[...], NORM_EPS).astype(BF16)

    j_prev = jnp.maximum(t - 1, 0) % n_col_tiles
    scale = jnp.where(j_prev < tiles_per_part, q_scale, 1.0)
    is_gate = j_prev >= 3 * tiles_per_part
    for cols in _col_chunks(o_ref.shape[1]):
        acc = raw_ref[:, cols]
        o_ref[:, cols] = jnp.where(is_gate, _silu(acc), acc * scale).astype(o_ref.dtype)
    for cols in _col_chunks(o_ref.shape[1]):
        raw_ref[:, cols] = jnp.dot(h_ref[...], w_ref[:, cols], preferred_element_type=F32)


def _attn_proj(x, gain, w, q_scale):
    s, d = x.shape
    n = w.shape[1]
    width = n // 4
    tm, tn = _tile(s, 1024), _tile(width, 512)
    n_col_tiles = n // tn
    n_tiles = (s // tm) * n_col_tiles
    kern = functools.partial(_attn_proj_kernel, n_col_tiles=n_col_tiles, q_scale=q_scale,
                             tiles_per_part=width // tn)

    def cur(t):
        return jnp.minimum(t, n_tiles - 1)

    def prev(t):
        return jnp.maximum(t - 1, 0)

    return pl.pallas_call(
        kern,
        grid=(n_tiles + 1,),
        in_specs=[pl.BlockSpec((tm, d), lambda t: (cur(t) // n_col_tiles, 0), pipeline_mode=pl.Buffered(1)),
                  pl.BlockSpec((1, d), lambda t: (0, 0)),
                  pl.BlockSpec((d, tn), lambda t: (0, cur(t) % n_col_tiles))],
        out_specs=pl.BlockSpec((tm, tn), lambda t: (prev(t) // n_col_tiles, prev(t) % n_col_tiles)),
        out_shape=jax.ShapeDtypeStruct((s, n), BF16),
        scratch_shapes=[pltpu.VMEM((tm, d), BF16), pltpu.VMEM((tm, tn), F32)],
        compiler_params=_params(1),
        name="attn_proj",
    )(x, gain.reshape(1, d), w)


_NT_DIMS = (((1,), (1,)), ((), ()))


def _diff_attn_kernel(rb_ref, lam_ref, gain_ref, q_ref, k_ref, v_ref, z_ref, o_ref,
                      bias_ref, sa_ref, sb_ref, ma_ref, mb_ref, m_ref, l_ref, acc_ref,
                      *, blk, head_dim, lam_init, thresholds):
    h = pl.program_id(0)
    i = pl.program_id(1)
    v_lane_tiles = acc_ref.shape[-1] // LANES

    @pl.when(i == 0)
    def _():
        row = lax.broadcasted_iota(jnp.int32, bias_ref.shape, 0)
        col = lax.broadcasted_iota(jnp.int32, bias_ref.shape, 1)
        dist = row - col + blk
        last = rb_ref[NUM_BUCKETS - 1, h]
        b = jnp.zeros(bias_ref.shape, F32)
        for bucket in range(NUM_BUCKETS - 2, -1, -1):
            val = (rb_ref[bucket, h] - last) * LOG2E
            b = jnp.where(dist < thresholds[bucket + 1], val, b)
        bias_ref[...] = jnp.where(dist >= 0, b, NEG_INF)

    m_ref[...] = jnp.full(m_ref.shape, -3e38, F32)
    l_ref[...] = jnp.zeros(l_ref.shape, F32)
    acc_ref[...] = jnp.zeros(acc_ref.shape, F32)

    q = q_ref[...]
    qs = (q[:, :head_dim], q[:, head_dim:])

    def lane_tiles_of(s):
        return [s[:, t * LANES:(t + 1) * LANES] for t in range(s.shape[1] // LANES)]

    def logits(start, n_keys, bias, dst):
        s_dst, max_dst = dst
        kj = k_ref[pl.ds(start, n_keys), :]
        for c in range(2):
            s = lax.dot_general(qs[c], kj[:, c * head_dim:(c + 1) * head_dim], _NT_DIMS,
                                preferred_element_type=F32)
            if bias is not None:
                s = s + bias
            s_dst[c, :, :n_keys] = s
            row_max = jnp.max(functools.reduce(jnp.maximum, lane_tiles_of(s)), axis=-1, keepdims=True)
            max_dst[c] = jnp.broadcast_to(row_max, (blk, LANES))

    def accumulate(start, n_keys, src):
        s_src, max_src = src
        vj = v_ref[pl.ds(start, n_keys), :]
        for c in range(2):
            tiles = lane_tiles_of(s_src[c, :, :n_keys])
            m_old = m_ref[c]
            m_new = jnp.maximum(m_old, max_src[c])
            alpha = jnp.exp2(m_old - m_new)
            ps = [jnp.exp2(t - m_new) for t in tiles]
            l_ref[c] = alpha * l_ref[c] + functools.reduce(jnp.add, ps)
            pv = jnp.dot(jnp.concatenate(ps, axis=1).astype(BF16), vj, preferred_element_type=F32)
            acc_ref[c] = jnp.concatenate([alpha] * v_lane_tiles, axis=1) * acc_ref[c] + pv
            m_ref[c] = m_new

    buf_a = (sa_ref, ma_ref)
    buf_b = (sb_ref, mb_ref)
    span = 2 * blk

    def span_start(u):
        return pl.multiple_of((i - 3 - 2 * u) * blk, blk)

    @pl.when(i == 0)
    def _():
        logits(0, blk, bias_ref[:, blk:], buf_a)
        accumulate(0, blk, buf_a)

    @pl.when(i >= 1)
    def _():
        logits(span_start(-1), span, bias_ref[...], buf_a)
        n_far = i - 1
        n_spans = n_far // 2
        odd_span = n_spans % 2 == 1
        odd_block = n_far % 2 == 1

        def two_spans(t):
            logits(span_start(2 * t), span, None, buf_b)
            accumulate(span_start(2 * t - 1), span, buf_a)
            logits(span_start(2 * t + 1), span, None, buf_a)
            accumulate(span_start(2 * t), span, buf_b)

        def four_spans(t, carry):
            two_spans(2 * t)
            two_spans(2 * t + 1)
            return carry

        lax.fori_loop(0, n_spans // 4, four_spans, 0)

        @pl.when(n_spans % 4 >= 2)
        def _():
            two_spans(2 * (n_spans // 4))

        pending = 2 * (n_spans // 2) - 1

        @pl.when(jnp.logical_not(odd_span) & jnp.logical_not(odd_block))
        def _():
            accumulate(span_start(pending), span, buf_a)

        @pl.when(jnp.logical_not(odd_span) & odd_block)
        def _():
            logits(0, blk, None, buf_b)
            accumulate(span_start(pending), span, buf_a)
            accumulate(0, blk, buf_b)

        @pl.when(odd_span & jnp.logical_not(odd_block))
        def _():
            logits(span_start(pending + 1), span, None, buf_b)
            accumulate(span_start(pending), span, buf_a)
            accumulate(span_start(pending + 1), span, buf_b)

        @pl.when(odd_span & odd_block)
        def _():
            logits(span_start(pending + 1), span, None, buf_b)
            accumulate(span_start(pending), span, buf_a)
            logits(0, blk, None, buf_a)
            accumulate(span_start(pending + 1), span, buf_b)
            accumulate(0, blk, buf_a)

    lp = lam_ref[...]
    lam = (jnp.exp(jnp.sum(lp[0:1] * lp[1:2], axis=-1, keepdims=True))
           - jnp.exp(jnp.sum(lp[2:3] * lp[3:4], axis=-1, keepdims=True)) + lam_init)
    inv_l = [1.0 / jnp.sum(l_ref[c], axis=-1, keepdims=True) for c in range(2)]
    o = acc_ref[0] * inv_l[0] - lam * (acc_ref[1] * inv_l[1])
    o = _rmsnorm(o, gain_ref[...], SUBLN_EPS) * (1.0 - lam_init)
    o_ref[...] = (o * z_ref[...].astype(F32)).astype(o_ref.dtype)


def _diff_attn(qkvz, rel_bias, lam_params, subln_gain, n_heads, lam_init):
    s = qkvz.shape[0]
    width = qkvz.shape[1] // 4
    v_dim = width // n_heads
    head_dim = v_dim // 2
    blk = _tile(s, 512)
    thresholds = _bucket_thresholds()
    assert thresholds[NUM_BUCKETS - 1] <= blk, "bias must be constant beyond the first sub-diagonal block"
    kern = functools.partial(_diff_attn_kernel, blk=blk, head_dim=head_dim, lam_init=lam_init,
                             thresholds=thresholds)
    return pl.pallas_call(
        kern,
        grid=(n_heads, s // blk),
        in_specs=[pl.BlockSpec(memory_space=pltpu.SMEM),
                  pl.BlockSpec(lam_params.shape, lambda h, i: (0, 0)),
                  pl.BlockSpec((1, v_dim), lambda h, i: (0, 0)),
                  pl.BlockSpec((blk, v_dim), lambda h, i: (i, h)),
                  pl.BlockSpec((s, v_dim), lambda h, i: (0, n_heads + h)),
                  pl.BlockSpec((s, v_dim), lambda h, i: (0, 2 * n_heads + h)),
                  pl.BlockSpec((blk, v_dim), lambda h, i: (i, 3 * n_heads + h))],
        out_specs=pl.BlockSpec((blk, v_dim), lambda h, i: (i, h)),
        out_shape=jax.ShapeDtypeStruct((s, width), BF16),
        scratch_shapes=[pltpu.VMEM((blk, 2 * blk), F32),
                        pltpu.VMEM((2, blk, 2 * blk), F32),
                        pltpu.VMEM((2, blk, 2 * blk), F32),
                        pltpu.VMEM((2, blk, LANES), F32),
                        pltpu.VMEM((2, blk, LANES), F32),
                        pltpu.VMEM((2, blk, LANES), F32),
                        pltpu.VMEM((2, blk, LANES), F32),
                        pltpu.VMEM((2, blk, v_dim), F32)],
        compiler_params=_params(2),
        name="diff_attn",
    )(rel_bias, lam_params, subln_gain.reshape(1, v_dim), qkvz, qkvz, qkvz, qkvz)


def _matmul_residual_kernel(a_ref, w_ref, r_ref, o_ref):
    for cols in _col_chunks(o_ref.shape[1]):
        o_ref[:, cols] = r_ref[:, cols] + jnp.dot(a_ref[...], w_ref[:, cols], preferred_element_type=F32)


def _matmul_residual(a, w, resid, name):
    s, k = a.shape
    n = w.shape[1]
    tm = _tile(s, 1024)
    tn = _tile(n, max(V7X_MXU_WIDTH, WEIGHT_TILE_ELEMS // k))
    return pl.pallas_call(
        _matmul_residual_kernel,
        grid=(s // tm, n // tn),
        in_specs=[pl.BlockSpec((tm, k), lambda i, j: (i, 0)),
                  pl.BlockSpec((k, tn), lambda i, j: (0, j)),
                  pl.BlockSpec((tm, tn), lambda i, j: (i, j))],
        out_specs=pl.BlockSpec((tm, tn), lambda i, j: (i, j)),
        out_shape=jax.ShapeDtypeStruct((s, n), F32),
        compiler_params=_params(2),
        name=name,
    )(a, w, resid)


def _pool_proj_kernel(x_ref, g_ref, w_ref, o_ref, h_ref, halo_ref, *, tm, n_u_tiles, tiles_per_group):
    i = pl.program_id(0)
    j = pl.program_id(1)

    @pl.when(j == 0)
    def _():
        h_ref[...] = _rmsnorm(x_ref[...], g_ref[...], NORM_EPS).astype(BF16)

    acc = jnp.dot(h_ref[...], w_ref[...], preferred_element_type=F32)

    @pl.when(j >= n_u_tiles)
    def _():
        o_ref[...] = _silu(acc).astype(o_ref.dtype)

    for g, window in enumerate(POOL_WINDOWS):
        @pl.when((j >= g * tiles_per_group) & (j < (g + 1) * tiles_per_group))
        def _(window=window):
            jj = jnp.minimum(j, n_u_tiles - 1)

            @pl.when(i == 0)
            def _():
                halo_ref[jj] = jnp.zeros(halo_ref.shape[1:], F32)

            total = jnp.concatenate([halo_ref[jj], acc], axis=0)
            shift = 1
            while shift < window:
                total = total + pltpu.roll(total, shift, axis=0)
                shift *= 2
            t = i * tm + lax.broadcasted_iota(jnp.int32, (tm, 1), 0)
            inv_cnt = 1.0 / jnp.minimum(t + 1, window).astype(F32)
            o_ref[...] = (total[POOL_HALO:] * inv_cnt - acc).astype(o_ref.dtype)
            halo_ref[jj] = acc[tm - POOL_HALO:]


def _pool_proj(x, gain, w, n_groups):
    s, d = x.shape
    n = w.shape[1]
    width = n // 2
    group_dim = width // n_groups
    tm, tn = _tile(s, 1024), _tile(group_dim, 512)
    n_u_tiles = width // tn
    kern = functools.partial(_pool_proj_kernel, tm=tm, n_u_tiles=n_u_tiles, tiles_per_group=group_dim // tn)
    return pl.pallas_call(
        kern,
        grid=(s // tm, n // tn),
        in_specs=[pl.BlockSpec((tm, d), lambda i, j: (i, 0), pipeline_mode=pl.Buffered(1)),
                  pl.BlockSpec((1, d), lambda i, j: (0, 0)),
                  pl.BlockSpec((d, tn), lambda i, j: (0, j))],
        out_specs=pl.BlockSpec((tm, tn), lambda i, j: (i, j)),
        out_shape=jax.ShapeDtypeStruct((s, n), BF16),
        scratch_shapes=[pltpu.VMEM((tm, d), BF16),
                        pltpu.VMEM((n_u_tiles, POOL_HALO, tn), F32)],
        compiler_params=_params(2),
        name="pool_proj",
    )(x, gain.reshape(1, d), w)


def _pool_group_kernel(p_ref, w_ref, s_ref, z_ref, o_ref):
    for cols in _col_chunks(o_ref.shape[1]):
        acc = jnp.dot(p_ref[...], w_ref[:, cols], preferred_element_type=F32)
        o_ref[:, cols] = (acc * s_ref[:, cols] * z_ref[:, cols].astype(F32)).astype(o_ref.dtype)


def _pool_group(pz, w_group, scale):
    s = pz.shape[0]
    n_groups, group_dim, _ = w_group.shape
    width = n_groups * group_dim
    tm, tn = _tile(s, 1024), _tile(group_dim, 2048)
    tpg = group_dim // tn
    return pl.pallas_call(
        _pool_group_kernel,
        grid=(s // tm, n_groups, tpg),
        in_specs=[pl.BlockSpec((tm, group_dim), lambda i, g, j: (i, g)),
                  pl.BlockSpec((None, group_dim, tn), lambda i, g, j: (g, 0, j)),
                  pl.BlockSpec((1, tn), lambda i, g, j: (0, g * tpg + j)),
                  pl.BlockSpec((tm, tn), lambda i, g, j: (i, (n_groups + g) * tpg + j))],
        out_specs=pl.BlockSpec((tm, tn), lambda i, g, j: (i, g * tpg + j)),
        out_shape=jax.ShapeDtypeStruct((s, width), BF16),
        compiler_params=_params(3),
        name="pool_group",
    )(pz, w_group, scale.reshape(1, width), pz)


def _final_norm_kernel(x_ref, g_ref, o_ref):
    o_ref[...] = _rmsnorm(x_ref[...], g_ref[...], NORM_EPS)


def _final_norm(x, gain):
    s, d = x.shape
    tm = _tile(s, 512)
    return pl.pallas_call(
        _final_norm_kernel,
        grid=(s // tm,),
        in_specs=[pl.BlockSpec((tm, d), lambda i: (i, 0)),
                  pl.BlockSpec((1, d), lambda i: (0, 0))],
        out_specs=pl.BlockSpec((tm, d), lambda i: (i, 0)),
        out_shape=jax.ShapeDtypeStruct((s, d), F32),
        compiler_params=_params(1),
        name="final_norm",
    )(x, gain.reshape(1, d))


def kernel(x, norm_gains, final_norm_gain, rel_bias, attn_w_in, attn_lambda, attn_subln_gain, attn_w_out,
           pool_w_in, pool_w_group, pool_scale, pool_w_out):
    b, s, d = x.shape
    assert b == 1, "attention and pooling tiles assume one sequence"
    assert norm_gains.shape[0] == 2 and attn_w_in.shape[0] == 1 and pool_w_in.shape[0] == 1
    n_heads = rel_bias.shape[1]
    head_dim = attn_w_in.shape[2] // (8 * n_heads)
    xs = x.reshape(s, d)

    qkvz = _attn_proj(xs, norm_gains[0], attn_w_in[0].astype(BF16), head_dim ** -0.5 * LOG2E)
    og = _diff_attn(qkvz, rel_bias, attn_lambda[0], attn_subln_gain[0], n_heads, _lambda_init(0))
    x1 = _matmul_residual(og, attn_w_out[0].astype(BF16), xs, "attn_out")

    pz = _pool_proj(x1, norm_gains[1], pool_w_in[0].astype(BF16), pool_w_group.shape[1])
    mixed = _pool_group(pz, pool_w_group[0].astype(BF16), pool_scale[0])
    x2 = _matmul_residual(mixed, pool_w_out[0].astype(BF16), x1, "pool_out")

    return _final_norm(x2, final_norm_gain).reshape(b, s, d)
```

```python
import functools
import math

import jax
import jax.numpy as jnp
import numpy as np
from jax import lax
from jax.experimental import pallas as pl
from jax.experimental.pallas import tpu as pltpu

NUM_BUCKETS = 32
MAX_DISTANCE = 128
POOL_WINDOWS = (2, 4, 8, 16)
assert all(w == 2 << k for k, w in enumerate(POOL_WINDOWS)), "pooling doubles a running window sum"
NORM_EPS = 1e-6
SUBLN_EPS = 1e-5
NEG_INF = -1e30
LOG2E = math.log2(math.e)

V7X_VMEM_LIMIT_BYTES = 60 * 1024 * 1024
LANES = 128
BF16_SUBLANES = 16
V7X_MXU_WIDTH = 256
WEIGHT_TILE_ELEMS = 2 * 1024 * 1024
POOL_HALO = max(POOL_WINDOWS)

BF16 = jnp.bfloat16
F32 = jnp.float32


def _lambda_init(layer_idx):
    return 0.8 - 0.6 * math.exp(-0.3 * layer_idx)


def _bucket_thresholds():
    max_exact = NUM_BUCKETS // 2
    n = np.arange(0, 2 * MAX_DISTANCE, dtype=np.float64)
    large = max_exact + np.floor(
        np.log(np.maximum(n, max_exact) / max_exact) / math.log(MAX_DISTANCE / max_exact) * (NUM_BUCKETS - max_exact))
    bucket = np.where(n < max_exact, n, np.minimum(large, NUM_BUCKETS - 1)).astype(np.int64)
    return [int(np.argmax(bucket >= b)) for b in range(NUM_BUCKETS)]


def _tile(dim, target):
    t = min(dim, target)
    while dim % t:
        t -= LANES
    assert t > 0
    return t


def _params(n_axes):
    return pltpu.CompilerParams(dimension_semantics=("arbitrary",) * n_axes,
                                vmem_limit_bytes=V7X_VMEM_LIMIT_BYTES)


def _rmsnorm(x, gain, eps):
    inv = lax.rsqrt(jnp.mean(x * x, axis=-1, keepdims=True) + eps)
    return x * inv * gain


def _silu(x):
    half = 0.5 * x
    return half + half * jnp.tanh(half)


def _col_chunks(n):
    step = V7X_MXU_WIDTH if n % V7X_MXU_WIDTH == 0 else n
    return [slice(c, c + step) for c in range(0, n, step)]


def _attn_proj_kernel(x_ref, g_ref, w_ref, o_ref, h_ref, raw_ref, *, n_col_tiles, q_scale, tiles_per_part):
    t = pl.program_id(0)
    n_tiles = pl.num_programs(0) - 1

    @pl.when(t == 0)
    def _():
        raw_ref[...] = jnp.zeros(raw_ref.shape, F32)

    @pl.when(jnp.minimum(t, n_tiles - 1) % n_col_tiles == 0)
    def _():
        h_ref[...] = _rmsnorm(x_ref[...], g_ref[...], NORM_EPS).astype(BF16)

    j_prev = jnp.maximum(t - 1, 0) % n_col_tiles
    scale = jnp.where(j_prev < tiles_per_part, q_scale, 1.0)
    is_gate = j_prev >= 3 * tiles_per_part
    for cols in _col_chunks(o_ref.shape[1]):
        acc = raw_ref[:, cols]
        o_ref[:, cols] = jnp.where(is_gate, _silu(acc), acc * scale).astype(o_ref.dtype)
    for cols in _col_chunks(o_ref.shape[1]):
        raw_ref[:, cols] = jnp.dot(h_ref[...], w_ref[:, cols], preferred_element_type=F32)


def _attn_proj(x, gain, w, q_scale):
    s, d = x.shape
    n = w.shape[1]
    width = n // 4
    tm, tn = _tile(s, 512), _tile(width, 1024)
    n_col_tiles = n // tn
    n_tiles = (s // tm) * n_col_tiles
    kern = functools.partial(_attn_proj_kernel, n_col_tiles=n_col_tiles, q_scale=q_scale,
                             tiles_per_part=width // tn)

    def cur(t):
        return jnp.minimum(t, n_tiles - 1)

    def prev(t):
        return jnp.maximum(t - 1, 0)

    return pl.pallas_call(
        kern,
        grid=(n_tiles + 1,),
        in_specs=[pl.BlockSpec((tm, d), lambda t: (cur(t) // n_col_tiles, 0)),
                  pl.BlockSpec((1, d), lambda t: (0, 0)),
                  pl.BlockSpec((d, tn), lambda t: (0, cur(t) % n_col_tiles))],
        out_specs=pl.BlockSpec((tm, tn), lambda t: (prev(t) // n_col_tiles, prev(t) % n_col_tiles)),
        out_shape=jax.ShapeDtypeStruct((s, n), BF16),
        scratch_shapes=[pltpu.VMEM((tm, d), BF16), pltpu.VMEM((tm, tn), F32)],
        compiler_params=_params(1),
        name="attn_proj",
    )(x, gain.reshape(1, d), w)


_NT_DIMS = (((1,), (1,)), ((), ()))


def _diff_attn_kernel(rb_ref, lam_ref, gain_ref, q_ref, k_ref, v_ref, z_ref, *refs,
                      blk, head_dim, lam_init, thresholds, n_riders):
    rider_in, (o_ref, *rider_out) = refs[:n_riders], refs[n_riders:2 * n_riders + 1]
    bias_ref, sa_ref, sb_ref, ma_ref, mb_ref, m_ref, l_ref, acc_ref = refs[2 * n_riders + 1:]
    for w_ref, wb_ref in zip(rider_in, rider_out):
        wb_ref[...] = w_ref[...].astype(BF16)

    h = pl.program_id(0)
    i = pl.program_id(1)
    v_lane_tiles = acc_ref.shape[-1] // LANES

    @pl.when(i == 0)
    def _():
        row = lax.broadcasted_iota(jnp.int32, bias_ref.shape, 0)
        col = lax.broadcasted_iota(jnp.int32, bias_ref.shape, 1)
        dist = row - col + blk
        last = rb_ref[NUM_BUCKETS - 1, h]
        b = jnp.zeros(bias_ref.shape, F32)
        for bucket in range(NUM_BUCKETS - 2, -1, -1):
            val = (rb_ref[bucket, h] - last) * LOG2E
            b = jnp.where(dist < thresholds[bucket + 1], val, b)
        bias_ref[...] = jnp.where(dist >= 0, b, NEG_INF)

    m_ref[...] = jnp.full(m_ref.shape, -3e38, F32)
    l_ref[...] = jnp.zeros(l_ref.shape, F32)
    acc_ref[...] = jnp.zeros(acc_ref.shape, F32)

    q = q_ref[...]
    qs = (q[:, :head_dim], q[:, head_dim:])

    def lane_tiles_of(s):
        return [s[:, t * LANES:(t + 1) * LANES] for t in range(s.shape[1] // LANES)]

    def logits(start, n_keys, bias, dst):
        s_dst, max_dst = dst
        kj = k_ref[pl.ds(start, n_keys), :]
        for c in range(2):
            s = lax.dot_general(qs[c], kj[:, c * head_dim:(c + 1) * head_dim], _NT_DIMS,
                                preferred_element_type=F32)
            if bias is not None:
                s = s + bias
            s_dst[c, :, :n_keys] = s
            row_max = jnp.max(functools.reduce(jnp.maximum, lane_tiles_of(s)), axis=-1, keepdims=True)
            max_dst[c] = jnp.broadcast_to(row_max, (blk, LANES))

    def accumulate(start, n_keys, src):
        s_src, max_src = src
        vj = v_ref[pl.ds(start, n_keys), :]
        for c in range(2):
            tiles = lane_tiles_of(s_src[c, :, :n_keys])
            m_old = m_ref[c]
            m_new = jnp.maximum(m_old, max_src[c])
            alpha = jnp.exp2(m_old - m_new)
            ps = [jnp.exp2(t - m_new) for t in tiles]
            l_ref[c] = alpha * l_ref[c] + functools.reduce(jnp.add, ps)
            pv = jnp.dot(jnp.concatenate(ps, axis=1).astype(BF16), vj, preferred_element_type=F32)
            acc_ref[c] = jnp.concatenate([alpha] * v_lane_tiles, axis=1) * acc_ref[c] + pv
            m_ref[c] = m_new

    buf_a = (sa_ref, ma_ref)
    buf_b = (sb_ref, mb_ref)
    span = 2 * blk

    def span_start(u):
        return pl.multiple_of((i - 3 - 2 * u) * blk, blk)

    @pl.when(i == 0)
    def _():
        logits(0, blk, bias_ref[:, blk:], buf_a)
        accumulate(0, blk, buf_a)

    @pl.when(i >= 1)
    def _():
        logits(span_start(-1), span, bias_ref[...], buf_a)
        n_far = i - 1
        n_spans = n_far // 2
        odd_span = n_spans % 2 == 1
        odd_block = n_far % 2 == 1

        def two_spans(t):
            logits(span_start(2 * t), span, None, buf_b)
            accumulate(span_start(2 * t - 1), span, buf_a)
            logits(span_start(2 * t + 1), span, None, buf_a)
            accumulate(span_start(2 * t), span, buf_b)

        def four_spans(t, carry):
            two_spans(2 * t)
            two_spans(2 * t + 1)
            return carry

        lax.fori_loop(0, n_spans // 4, four_spans, 0)

        @pl.when(n_spans % 4 >= 2)
        def _():
            two_spans(2 * (n_spans // 4))

        pending = 2 * (n_spans // 2) - 1

        @pl.when(jnp.logical_not(odd_span) & jnp.logical_not(odd_block))
        def _():
            accumulate(span_start(pending), span, buf_a)

        @pl.when(jnp.logical_not(odd_span) & odd_block)
        def _():
            logits(0, blk, None, buf_b)
            accumulate(span_start(pending), span, buf_a)
            accumulate(0, blk, buf_b)

        @pl.when(odd_span & jnp.logical_not(odd_block))
        def _():
            logits(span_start(pending + 1), span, None, buf_b)
            accumulate(span_start(pending), span, buf_a)
            accumulate(span_start(pending + 1), span, buf_b)

        @pl.when(odd_span & odd_block)
        def _():
            logits(span_start(pending + 1), span, None, buf_b)
            accumulate(span_start(pending), span, buf_a)
            logits(0, blk, None, buf_a)
            accumulate(span_start(pending + 1), span, buf_b)
            accumulate(0, blk, buf_a)

    lp = lam_ref[...]
    lam = (jnp.exp(jnp.sum(lp[0:1] * lp[1:2], axis=-1, keepdims=True))
           - jnp.exp(jnp.sum(lp[2:3] * lp[3:4], axis=-1, keepdims=True)) + lam_init)
    inv_l = [1.0 / jnp.sum(l_ref[c], axis=-1, keepdims=True) for c in range(2)]
    o = acc_ref[0] * inv_l[0] - lam * (acc_ref[1] * inv_l[1])
    o = _rmsnorm(o, gain_ref[...], SUBLN_EPS) * (1.0 - lam_init)
    o_ref[...] = (o * z_ref[...].astype(F32)).astype(o_ref.dtype)


def _rider_spec(w, n_steps, n_q):
    rows, cols = w.shape
    n_blocks = next(nb for nb in range(min(n_steps, rows // BF16_SUBLANES), 0, -1)
                    if rows % nb == 0 and (rows // nb) % BF16_SUBLANES == 0)
    return pl.BlockSpec((rows // n_blocks, cols), lambda h, i: (jnp.minimum(h * n_q + i, n_blocks - 1), 0))


def _diff_attn(qkvz, rel_bias, lam_params, subln_gain, n_heads, lam_init, rider_weights):
    s = qkvz.shape[0]
    width = qkvz.shape[1] // 4
    v_dim = width // n_heads
    head_dim = v_dim // 2
    blk = _tile(s, 512)
    n_q = s // blk
    thresholds = _bucket_thresholds()
    assert thresholds[NUM_BUCKETS - 1] <= blk, "bias must be constant beyond the first sub-diagonal block"
    kern = functools.partial(_diff_attn_kernel, blk=blk, head_dim=head_dim, lam_init=lam_init,
                             thresholds=thresholds, n_riders=len(rider_weights))
    rider_specs = [_rider_spec(w, n_heads * n_q, n_q) for w in rider_weights]
    return pl.pallas_call(
        kern,
        grid=(n_heads, n_q),
        in_specs=[pl.BlockSpec(memory_space=pltpu.SMEM),
                  pl.BlockSpec(lam_params.shape, lambda h, i: (0, 0)),
                  pl.BlockSpec((1, v_dim), lambda h, i: (0, 0)),
                  pl.BlockSpec((blk, v_dim), lambda h, i: (i, h)),
                  pl.BlockSpec((s, v_dim), lambda h, i: (0, n_heads + h)),
                  pl.BlockSpec((s, v_dim), lambda h, i: (0, 2 * n_heads + h)),
                  pl.BlockSpec((blk, v_dim), lambda h, i: (i, 3 * n_heads + h))] + rider_specs,
        out_specs=[pl.BlockSpec((blk, v_dim), lambda h, i: (i, h))] + rider_specs,
        out_shape=[jax.ShapeDtypeStruct((s, width), BF16)]
        + [jax.ShapeDtypeStruct(w.shape, BF16) for w in rider_weights],
        scratch_shapes=[pltpu.VMEM((blk, 2 * blk), F32),
                        pltpu.VMEM((2, blk, 2 * blk), F32),
                        pltpu.VMEM((2, blk, 2 * blk), F32),
                        pltpu.VMEM((2, blk, LANES), F32),
                        pltpu.VMEM((2, blk, LANES), F32),
                        pltpu.VMEM((2, blk, LANES), F32),
                        pltpu.VMEM((2, blk, LANES), F32),
                        pltpu.VMEM((2, blk, v_dim), F32)],
        compiler_params=_params(2),
        name="diff_attn",
    )(rel_bias, lam_params, subln_gain.reshape(1, v_dim), qkvz, qkvz, qkvz, qkvz, *rider_weights)


def _matmul_residual_kernel(a_ref, w_ref, r_ref, o_ref):
    for cols in _col_chunks(o_ref.shape[1]):
        o_ref[:, cols] = r_ref[:, cols] + jnp.dot(a_ref[...], w_ref[:, cols], preferred_element_type=F32)


def _matmul_residual(a, w, resid, name):
    s, k = a.shape
    n = w.shape[1]
    tm = _tile(s, 1024)
    tn = _tile(n, max(V7X_MXU_WIDTH, WEIGHT_TILE_ELEMS // k))
    return pl.pallas_call(
        _matmul_residual_kernel,
        grid=(s // tm, n // tn),
        in_specs=[pl.BlockSpec((tm, k), lambda i, j: (i, 0)),
                  pl.BlockSpec((k, tn), lambda i, j: (0, j)),
                  pl.BlockSpec((tm, tn), lambda i, j: (i, j))],
        out_specs=pl.BlockSpec((tm, tn), lambda i, j: (i, j)),
        out_shape=jax.ShapeDtypeStruct((s, n), F32),
        compiler_params=_params(2),
        name=name,
    )(a, w, resid)


def _pool_proj_kernel(x_ref, g_ref, w_ref, o_ref, h_ref, halo_ref, *, tm, n_u_tiles, tiles_per_group):
    i = pl.program_id(0)
    j = pl.program_id(1)

    @pl.when(j == 0)
    def _():
        h_ref[...] = _rmsnorm(x_ref[...], g_ref[...], NORM_EPS).astype(BF16)

    acc = jnp.dot(h_ref[...], w_ref[...], preferred_element_type=F32)

    @pl.when(j >= n_u_tiles)
    def _():
        o_ref[...] = _silu(acc).astype(o_ref.dtype)

    for g, window in enumerate(POOL_WINDOWS):
        @pl.when((j >= g * tiles_per_group) & (j < (g + 1) * tiles_per_group))
        def _(window=window):
            jj = jnp.minimum(j, n_u_tiles - 1)

            @pl.when(i == 0)
            def _():
                halo_ref[jj] = jnp.zeros(halo_ref.shape[1:], F32)

            total = jnp.concatenate([halo_ref[jj], acc], axis=0)
            shift = 1
            while shift < window:
                total = total + pltpu.roll(total, shift, axis=0)
                shift *= 2
            t = i * tm + lax.broadcasted_iota(jnp.int32, (tm, 1), 0)
            inv_cnt = 1.0 / jnp.minimum(t + 1, window).astype(F32)
            o_ref[...] = (total[POOL_HALO:] * inv_cnt - acc).astype(o_ref.dtype)
            halo_ref[jj] = acc[tm - POOL_HALO:]


def _pool_proj(x, gain, w, n_groups):
    s, d = x.shape
    n = w.shape[1]
    width = n // 2
    group_dim = width // n_groups
    tm, tn = _tile(s, 512), _tile(group_dim, 1024)
    n_u_tiles = width // tn
    kern = functools.partial(_pool_proj_kernel, tm=tm, n_u_tiles=n_u_tiles, tiles_per_group=group_dim // tn)
    return pl.pallas_call(
        kern,
        grid=(s // tm, n // tn),
        in_specs=[pl.BlockSpec((tm, d), lambda i, j: (i, 0)),
                  pl.BlockSpec((1, d), lambda i, j: (0, 0)),
                  pl.BlockSpec((d, tn), lambda i, j: (0, j))],
        out_specs=pl.BlockSpec((tm, tn), lambda i, j: (i, j)),
        out_shape=jax.ShapeDtypeStruct((s, n), BF16),
        scratch_shapes=[pltpu.VMEM((tm, d), BF16),
                        pltpu.VMEM((n_u_tiles, POOL_HALO, tn), F32)],
        compiler_params=_params(2),
        name="pool_proj",
    )(x, gain.reshape(1, d), w)


def _pool_group_kernel(p_ref, w_ref, s_ref, z_ref, o_ref):
    for cols in _col_chunks(o_ref.shape[1]):
        acc = jnp.dot(p_ref[...], w_ref[:, cols], preferred_element_type=F32)
        o_ref[:, cols] = (acc * s_ref[:, cols] * z_ref[:, cols].astype(F32)).astype(o_ref.dtype)


def _pool_group(pz, w_group, scale):
    s = pz.shape[0]
    n_groups, group_dim, _ = w_group.shape
    width = n_groups * group_dim
    tm, tn = _tile(s, 1024), _tile(group_dim, 2048)
    tpg = group_dim // tn
    return pl.pallas_call(
        _pool_group_kernel,
        grid=(s // tm, n_groups, tpg),
        in_specs=[pl.BlockSpec((tm, group_dim), lambda i, g, j: (i, g)),
                  pl.BlockSpec((None, group_dim, tn), lambda i, g, j: (g, 0, j)),
                  pl.BlockSpec((1, tn), lambda i, g, j: (0, g * tpg + j)),
                  pl.BlockSpec((tm, tn), lambda i, g, j: (i, (n_groups + g) * tpg + j))],
        out_specs=pl.BlockSpec((tm, tn), lambda i, g, j: (i, g * tpg + j)),
        out_shape=jax.ShapeDtypeStruct((s, width), BF16),
        compiler_params=_params(3),
        name="pool_group",
    )(pz, w_group, scale.reshape(1, width), pz)


def _final_norm_kernel(x_ref, g_ref, o_ref):
    o_ref[...] = _rmsnorm(x_ref[...], g_ref[...], NORM_EPS)


def _final_norm(x, gain):
    s, d = x.shape
    tm = _tile(s, 512)
    return pl.pallas_call(
        _final_norm_kernel,
        grid=(s // tm,),
        in_specs=[pl.BlockSpec((tm, d), lambda i: (i, 0)),
                  pl.BlockSpec((1, d), lambda i: (0, 0))],
        out_specs=pl.BlockSpec((tm, d), lambda i: (i, 0)),
        out_shape=jax.ShapeDtypeStruct((s, d), F32),
        compiler_params=_params(1),
        name="final_norm",
    )(x, gain.reshape(1, d))


def kernel(x, norm_gains, final_norm_gain, rel_bias, attn_w_in, attn_lambda, attn_subln_gain, attn_w_out,
           pool_w_in, pool_w_group, pool_scale, pool_w_out):
    b, s, d = x.shape
    assert b == 1, "attention and pooling tiles assume one sequence"
    assert norm_gains.shape[0] == 2 and attn_w_in.shape[0] == 1 and pool_w_in.shape[0] == 1
    n_heads = rel_bias.shape[1]
    head_dim = attn_w_in.shape[2] // (8 * n_heads)
    xs = x.reshape(s, d)

    n_groups, group_dim = pool_w_group.shape[1:3]
    riders = [attn_w_out[0], pool_w_in[0], pool_w_group[0].reshape(n_groups * group_dim, group_dim), pool_w_out[0]]

    qkvz = _attn_proj(xs, norm_gains[0], attn_w_in[0].astype(BF16), head_dim ** -0.5 * LOG2E)
    og, w_attn_out, w_pool_in, w_pool_group, w_pool_out = _diff_attn(
        qkvz, rel_bias, attn_lambda[0], attn_subln_gain[0], n_heads, _lambda_init(0), riders)
    x1 = _matmul_residual(og, w_attn_out, xs, "attn_out")

    pz = _pool_proj(x1, norm_gains[1], w_pool_in, n_groups)
    mixed = _pool_group(pz, w_pool_group.reshape(n_groups, group_dim, group_dim), pool_scale[0])
    x2 = _matmul_residual(mixed, w_pool_out, x1, "pool_out")

    return _final_norm(x2, final_norm_gain).reshape(b, s, d)
```

```python
import functools
import math

import jax
import jax.numpy as jnp
import numpy as np
from jax import lax
from jax.experimental import pallas as pl
from jax.experimental.pallas import tpu as pltpu

NUM_BUCKETS = 32
MAX_DISTANCE = 128
POOL_WINDOWS = (2, 4, 8, 16)
assert all(w == 2 << k for k, w in enumerate(POOL_WINDOWS)), "pooling doubles a running window sum"
NORM_EPS = 1e-6
SUBLN_EPS = 1e-5
NEG_INF = -1e30
LOG2E = math.log2(math.e)

V7X_VMEM_LIMIT_BYTES = 60 * 1024 * 1024
LANES = 128
BF16_SUBLANES = 16
V7X_MXU_WIDTH = 256
WEIGHT_TILE_ELEMS = 2 * 1024 * 1024
POOL_HALO = max(POOL_WINDOWS)

BF16 = jnp.bfloat16
F32 = jnp.float32


def _lambda_init(layer_idx):
    return 0.8 - 0.6 * math.exp(-0.3 * layer_idx)


def _bucket_thresholds():
    max_exact = NUM_BUCKETS // 2
    n = np.arange(0, 2 * MAX_DISTANCE, dtype=np.float64)
    large = max_exact + np.floor(
        np.log(np.maximum(n, max_exact) / max_exact) / math.log(MAX_DISTANCE / max_exact) * (NUM_BUCKETS - max_exact))
    bucket = np.where(n < max_exact, n, np.minimum(large, NUM_BUCKETS - 1)).astype(np.int64)
    return [int(np.argmax(bucket >= b)) for b in range(NUM_BUCKETS)]


def _tile(dim, target):
    t = min(dim, target)
    while dim % t:
        t -= LANES
    assert t > 0
    return t


def _params(n_axes):
    return pltpu.CompilerParams(dimension_semantics=("arbitrary",) * n_axes,
                                vmem_limit_bytes=V7X_VMEM_LIMIT_BYTES)


def _rmsnorm(x, gain, eps):
    inv = lax.rsqrt(jnp.mean(x * x, axis=-1, keepdims=True) + eps)
    return x * inv * gain


def _silu(x):
    half = 0.5 * x
    return half + half * jnp.tanh(half)


def _col_chunks(n):
    step = V7X_MXU_WIDTH if n % V7X_MXU_WIDTH == 0 else n
    return [slice(c, c + step) for c in range(0, n, step)]


def _attn_proj_kernel(x_ref, g_ref, w_ref, o_ref, h_ref, raw_ref, *, n_col_tiles, q_scale, tiles_per_part):
    t = pl.program_id(0)
    n_tiles = pl.num_programs(0) - 1

    @pl.when(t == 0)
    def _():
        raw_ref[...] = jnp.zeros(raw_ref.shape, F32)

    @pl.when(jnp.minimum(t, n_tiles - 1) % n_col_tiles == 0)
    def _():
        h_ref[...] = _rmsnorm(x_ref[...], g_ref[...], NORM_EPS).astype(BF16)

    j_prev = jnp.maximum(t - 1, 0) % n_col_tiles
    scale = jnp.where(j_prev < tiles_per_part, q_scale, 1.0)
    is_gate = j_prev >= 3 * tiles_per_part
    for cols in _col_chunks(o_ref.shape[1]):
        acc = raw_ref[:, cols]
        o_ref[:, cols] = jnp.where(is_gate, _silu(acc), acc * scale).astype(o_ref.dtype)
    for cols in _col_chunks(o_ref.shape[1]):
        raw_ref[:, cols] = jnp.dot(h_ref[...], w_ref[:, cols], preferred_element_type=F32)


def _attn_proj(x, gain, w, q_scale):
    s, d = x.shape
    n = w.shape[1]
    width = n // 4
    tm, tn = _tile(s, 512), _tile(width, 1024)
    n_col_tiles = n // tn
    n_tiles = (s // tm) * n_col_tiles
    kern = functools.partial(_attn_proj_kernel, n_col_tiles=n_col_tiles, q_scale=q_scale,
                             tiles_per_part=width // tn)

    def cur(t):
        return jnp.minimum(t, n_tiles - 1)

    def prev(t):
        return jnp.maximum(t - 1, 0)

    return pl.pallas_call(
        kern,
        grid=(n_tiles + 1,),
        in_specs=[pl.BlockSpec((tm, d), lambda t: (cur(t) // n_col_tiles, 0)),
                  pl.BlockSpec((1, d), lambda t: (0, 0)),
                  pl.BlockSpec((d, tn), lambda t: (0, cur(t) % n_col_tiles))],
        out_specs=pl.BlockSpec((tm, tn), lambda t: (prev(t) // n_col_tiles, prev(t) % n_col_tiles)),
        out_shape=jax.ShapeDtypeStruct((s, n), BF16),
        scratch_shapes=[pltpu.VMEM((tm, d), BF16), pltpu.VMEM((tm, tn), F32)],
        compiler_params=_params(1),
        name="attn_proj",
    )(x, gain.reshape(1, d), w)


_NT_DIMS = (((1,), (1,)), ((), ()))


def _diff_attn_kernel(rb_ref, lam_ref, gain_ref, q_ref, k_ref, v_ref, z_ref, *refs,
                      blk, head_dim, lam_init, thresholds, n_riders):
    rider_in, (o_ref, *rider_out) = refs[:n_riders], refs[n_riders:2 * n_riders + 1]
    bias_ref, sa_ref, sb_ref, ma_ref, mb_ref, m_ref, l_ref, acc_ref = refs[2 * n_riders + 1:]

    h = pl.program_id(0)
    i = pl.program_id(1)
    v_lane_tiles = acc_ref.shape[-1] // LANES

    @pl.when(i == 0)
    def _():
        row = lax.broadcasted_iota(jnp.int32, bias_ref.shape, 0)
        col = lax.broadcasted_iota(jnp.int32, bias_ref.shape, 1)
        dist = row - col + blk
        last = rb_ref[NUM_BUCKETS - 1, h]
        b = jnp.zeros(bias_ref.shape, F32)
        for bucket in range(NUM_BUCKETS - 2, -1, -1):
            val = (rb_ref[bucket, h] - last) * LOG2E
            b = jnp.where(dist < thresholds[bucket + 1], val, b)
        bias_ref[...] = jnp.where(dist >= 0, b, NEG_INF)

    m_ref[...] = jnp.full(m_ref.shape, -3e38, F32)
    l_ref[...] = jnp.zeros(l_ref.shape, F32)
    acc_ref[...] = jnp.zeros(acc_ref.shape, F32)

    q = q_ref[...]
    qs = (q[:, :head_dim], q[:, head_dim:])

    def lane_tiles_of(s):
        return [s[:, t * LANES:(t + 1) * LANES] for t in range(s.shape[1] // LANES)]

    def logits(start, n_keys, bias, dst):
        s_dst, max_dst = dst
        kj = k_ref[pl.ds(start, n_keys), :]
        for c in range(2):
            s = lax.dot_general(qs[c], kj[:, c * head_dim:(c + 1) * head_dim], _NT_DIMS,
                                preferred_element_type=F32)
            if bias is not None:
                s = s + bias
            s_dst[c, :, :n_keys] = s
            row_max = jnp.max(functools.reduce(jnp.maximum, lane_tiles_of(s)), axis=-1, keepdims=True)
            max_dst[c] = jnp.broadcast_to(row_max, (blk, LANES))

    def accumulate(start, n_keys, src):
        s_src, max_src = src
        vj = v_ref[pl.ds(start, n_keys), :]
        for c in range(2):
            tiles = lane_tiles_of(s_src[c, :, :n_keys])
            m_old = m_ref[c]
            m_new = jnp.maximum(m_old, max_src[c])
            alpha = jnp.exp2(m_old - m_new)
            ps = [jnp.exp2(t - m_new) for t in tiles]
            l_ref[c] = alpha * l_ref[c] + functools.reduce(jnp.add, ps)
            pv = jnp.dot(jnp.concatenate(ps, axis=1).astype(BF16), vj, preferred_element_type=F32)
            acc_ref[c] = jnp.concatenate([alpha] * v_lane_tiles, axis=1) * acc_ref[c] + pv
            m_ref[c] = m_new

    buf_a = (sa_ref, ma_ref)
    buf_b = (sb_ref, mb_ref)
    span = 2 * blk

    def span_start(u):
        return pl.multiple_of((i - 3 - 2 * u) * blk, blk)

    @pl.when(i == 0)
    def _():
        logits(0, blk, bias_ref[:, blk:], buf_a)
        accumulate(0, blk, buf_a)

    @pl.when(i >= 1)
    def _():
        logits(span_start(-1), span, bias_ref[...], buf_a)
        n_far = i - 1
        n_spans = n_far // 2
        odd_span = n_spans % 2 == 1
        odd_block = n_far % 2 == 1

        def two_spans(t):
            logits(span_start(2 * t), span, None, buf_b)
            accumulate(span_start(2 * t - 1), span, buf_a)
            logits(span_start(2 * t + 1), span, None, buf_a)
            accumulate(span_start(2 * t), span, buf_b)

        def four_spans(t, carry):
            two_spans(2 * t)
            two_spans(2 * t + 1)
            return carry

        lax.fori_loop(0, n_spans // 4, four_spans, 0)

        @pl.when(n_spans % 4 >= 2)
        def _():
            two_spans(2 * (n_spans // 4))

        pending = 2 * (n_spans // 2) - 1

        @pl.when(jnp.logical_not(odd_span) & jnp.logical_not(odd_block))
        def _():
            accumulate(span_start(pending), span, buf_a)

        @pl.when(jnp.logical_not(odd_span) & odd_block)
        def _():
            logits(0, blk, None, buf_b)
            accumulate(span_start(pending), span, buf_a)
            accumulate(0, blk, buf_b)

        @pl.when(odd_span & jnp.logical_not(odd_block))
        def _():
            logits(span_start(pending + 1), span, None, buf_b)
            accumulate(span_start(pending), span, buf_a)
            accumulate(span_start(pending + 1), span, buf_b)

        @pl.when(odd_span & odd_block)
        def _():
            logits(span_start(pending + 1), span, None, buf_b)
            accumulate(span_start(pending), span, buf_a)
            logits(0, blk, None, buf_a)
            accumulate(span_start(pending + 1), span, buf_b)
            accumulate(0, blk, buf_a)

    for w_ref, wb_ref in zip(rider_in, rider_out):
        wb_ref[...] = w_ref[...].astype(BF16)

    lp = lam_ref[...]
    lam = (jnp.exp(jnp.sum(lp[0:1] * lp[1:2], axis=-1, keepdims=True))
           - jnp.exp(jnp.sum(lp[2:3] * lp[3:4], axis=-1, keepdims=True)) + lam_init)
    inv_l = [1.0 / jnp.sum(l_ref[c], axis=-1, keepdims=True) for c in range(2)]
    o = acc_ref[0] * inv_l[0] - lam * (acc_ref[1] * inv_l[1])
    o = _rmsnorm(o, gain_ref[...], SUBLN_EPS) * (1.0 - lam_init)
    o_ref[...] = (o * z_ref[...].astype(F32)).astype(o_ref.dtype)


def _rider_spec(w, n_steps, n_q):
    rows, cols = w.shape
    n_blocks = next(nb for nb in range(min(n_steps, rows // BF16_SUBLANES), 0, -1)
                    if rows % nb == 0 and (rows // nb) % BF16_SUBLANES == 0)
    return pl.BlockSpec((rows // n_blocks, cols), lambda h, i: (jnp.minimum(h * n_q + i, n_blocks - 1), 0))


def _diff_attn(qkvz, rel_bias, lam_params, subln_gain, n_heads, lam_init, rider_weights):
    s = qkvz.shape[0]
    width = qkvz.shape[1] // 4
    v_dim = width // n_heads
    head_dim = v_dim // 2
    blk = _tile(s, 512)
    n_q = s // blk
    thresholds = _bucket_thresholds()
    assert thresholds[NUM_BUCKETS - 1] <= blk, "bias must be constant beyond the first sub-diagonal block"
    kern = functools.partial(_diff_attn_kernel, blk=blk, head_dim=head_dim, lam_init=lam_init,
                             thresholds=thresholds, n_riders=len(rider_weights))
    rider_specs = [_rider_spec(w, n_heads * n_q, n_q) for w in rider_weights]
    return pl.pallas_call(
        kern,
        grid=(n_heads, n_q),
        in_specs=[pl.BlockSpec(memory_space=pltpu.SMEM),
                  pl.BlockSpec(lam_params.shape, lambda h, i: (0, 0)),
                  pl.BlockSpec((1, v_dim), lambda h, i: (0, 0)),
                  pl.BlockSpec((blk, v_dim), lambda h, i: (i, h)),
                  pl.BlockSpec((s, v_dim), lambda h, i: (0, n_heads + h)),
                  pl.BlockSpec((s, v_dim), lambda h, i: (0, 2 * n_heads + h)),
                  pl.BlockSpec((blk, v_dim), lambda h, i: (i, 3 * n_heads + h))] + rider_specs,
        out_specs=[pl.BlockSpec((blk, v_dim), lambda h, i: (i, h))] + rider_specs,
        out_shape=[jax.ShapeDtypeStruct((s, width), BF16)]
        + [jax.ShapeDtypeStruct(w.shape, BF16) for w in rider_weights],
        scratch_shapes=[pltpu.VMEM((blk, 2 * blk), F32),
                        pltpu.VMEM((2, blk, 2 * blk), F32),
                        pltpu.VMEM((2, blk, 2 * blk), F32),
                        pltpu.VMEM((2, blk, LANES), F32),
                        pltpu.VMEM((2, blk, LANES), F32),
                        pltpu.VMEM((2, blk, LANES), F32),
                        pltpu.VMEM((2, blk, LANES), F32),
                        pltpu.VMEM((2, blk, v_dim), F32)],
        compiler_params=_params(2),
        name="diff_attn",
    )(rel_bias, lam_params, subln_gain.reshape(1, v_dim), qkvz, qkvz, qkvz, qkvz, *rider_weights)


def _matmul_residual_kernel(a_ref, w_ref, r_ref, o_ref):
    for cols in _col_chunks(o_ref.shape[1]):
        o_ref[:, cols] = r_ref[:, cols] + jnp.dot(a_ref[...], w_ref[:, cols], preferred_element_type=F32)


def _matmul_residual(a, w, resid, name):
    s, k = a.shape
    n = w.shape[1]
    tm = _tile(s, 1024)
    tn = _tile(n, max(V7X_MXU_WIDTH, WEIGHT_TILE_ELEMS // k))
    return pl.pallas_call(
        _matmul_residual_kernel,
        grid=(s // tm, n // tn),
        in_specs=[pl.BlockSpec((tm, k), lambda i, j: (i, 0)),
                  pl.BlockSpec((k, tn), lambda i, j: (0, j)),
                  pl.BlockSpec((tm, tn), lambda i, j: (i, j))],
        out_specs=pl.BlockSpec((tm, tn), lambda i, j: (i, j)),
        out_shape=jax.ShapeDtypeStruct((s, n), F32),
        compiler_params=_params(2),
        name=name,
    )(a, w, resid)


def _pool_proj_kernel(x_ref, g_ref, w_ref, o_ref, h_ref, halo_ref, *, tm, n_u_tiles, tiles_per_group):
    i = pl.program_id(0)
    j = pl.program_id(1)

    @pl.when(j == 0)
    def _():
        h_ref[...] = _rmsnorm(x_ref[...], g_ref[...], NORM_EPS).astype(BF16)

    acc = jnp.dot(h_ref[...], w_ref[...], preferred_element_type=F32)

    @pl.when(j >= n_u_tiles)
    def _():
        o_ref[...] = _silu(acc).astype(o_ref.dtype)

    for g, window in enumerate(POOL_WINDOWS):
        @pl.when((j >= g * tiles_per_group) & (j < (g + 1) * tiles_per_group))
        def _(window=window):
            jj = jnp.minimum(j, n_u_tiles - 1)

            @pl.when(i == 0)
            def _():
                halo_ref[jj] = jnp.zeros(halo_ref.shape[1:], F32)

            total = jnp.concatenate([halo_ref[jj], acc], axis=0)
            shift = 1
            while shift < window:
                total = total + pltpu.roll(total, shift, axis=0)
                shift *= 2
            t = i * tm + lax.broadcasted_iota(jnp.int32, (tm, 1), 0)
            inv_cnt = 1.0 / jnp.minimum(t + 1, window).astype(F32)
            o_ref[...] = (total[POOL_HALO:] * inv_cnt - acc).astype(o_ref.dtype)
            halo_ref[jj] = acc[tm - POOL_HALO:]


def _pool_proj(x, gain, w, n_groups):
    s, d = x.shape
    n = w.shape[1]
    width = n // 2
    group_dim = width // n_groups
    tm, tn = _tile(s, 512), _tile(group_dim, 1024)
    n_u_tiles = width // tn
    kern = functools.partial(_pool_proj_kernel, tm=tm, n_u_tiles=n_u_tiles, tiles_per_group=group_dim // tn)
    return pl.pallas_call(
        kern,
        grid=(s // tm, n // tn),
        in_specs=[pl.BlockSpec((tm, d), lambda i, j: (i, 0)),
                  pl.BlockSpec((1, d), lambda i, j: (0, 0)),
                  pl.BlockSpec((d, tn), lambda i, j: (0, j))],
        out_specs=pl.BlockSpec((tm, tn), lambda i, j: (i, j)),
        out_shape=jax.ShapeDtypeStruct((s, n), BF16),
        scratch_shapes=[pltpu.VMEM((tm, d), BF16),
                        pltpu.VMEM((n_u_tiles, POOL_HALO, tn), F32)],
        compiler_params=_params(2),
        name="pool_proj",
    )(x, gain.reshape(1, d), w)


def _pool_group_kernel(p_ref, w_ref, s_ref, z_ref, o_ref):
    for cols in _col_chunks(o_ref.shape[1]):
        acc = jnp.dot(p_ref[...], w_ref[:, cols], preferred_element_type=F32)
        o_ref[:, cols] = (acc * s_ref[:, cols] * z_ref[:, cols].astype(F32)).astype(o_ref.dtype)


def _pool_group(pz, w_group, scale):
    s = pz.shape[0]
    n_groups, group_dim, _ = w_group.shape
    width = n_groups * group_dim
    tm, tn = _tile(s, 1024), _tile(group_dim, 2048)
    tpg = group_dim // tn
    return pl.pallas_call(
        _pool_group_kernel,
        grid=(s // tm, n_groups, tpg),
        in_specs=[pl.BlockSpec((tm, group_dim), lambda i, g, j: (i, g)),
                  pl.BlockSpec((None, group_dim, tn), lambda i, g, j: (g, 0, j)),
                  pl.BlockSpec((1, tn), lambda i, g, j: (0, g * tpg + j)),
                  pl.BlockSpec((tm, tn), lambda i, g, j: (i, (n_groups + g) * tpg + j))],
        out_specs=pl.BlockSpec((tm, tn), lambda i, g, j: (i, g * tpg + j)),
        out_shape=jax.ShapeDtypeStruct((s, width), BF16),
        compiler_params=_params(3),
        name="pool_group",
    )(pz, w_group, scale.reshape(1, width), pz)


def _final_norm_kernel(x_ref, g_ref, o_ref):
    o_ref[...] = _rmsnorm(x_ref[...], g_ref[...], NORM_EPS)


def _final_norm(x, gain):
    s, d = x.shape
    tm = _tile(s, 512)
    return pl.pallas_call(
        _final_norm_kernel,
        grid=(s // tm,),
        in_specs=[pl.BlockSpec((tm, d), lambda i: (i, 0)),
                  pl.BlockSpec((1, d), lambda i: (0, 0))],
        out_specs=pl.BlockSpec((tm, d), lambda i: (i, 0)),
        out_shape=jax.ShapeDtypeStruct((s, d), F32),
        compiler_params=_params(1),
        name="final_norm",
    )(x, gain.reshape(1, d))


def kernel(x, norm_gains, final_norm_gain, rel_bias, attn_w_in, attn_lambda, attn_subln_gain, attn_w_out,
           pool_w_in, pool_w_group, pool_scale, pool_w_out):
    b, s, d = x.shape
    assert b == 1, "attention and pooling tiles assume one sequence"
    assert norm_gains.shape[0] == 2 and attn_w_in.shape[0] == 1 and pool_w_in.shape[0] == 1
    n_heads = rel_bias.shape[1]
    head_dim = attn_w_in.shape[2] // (8 * n_heads)
    xs = x.reshape(s, d)

    n_groups, group_dim = pool_w_group.shape[1:3]
    riders = [attn_w_out[0], pool_w_in[0], pool_w_group[0].reshape(n_groups * group_dim, group_dim), pool_w_out[0]]

    qkvz = _attn_proj(xs, norm_gains[0], attn_w_in[0].astype(BF16), head_dim ** -0.5 * LOG2E)
    og, w_attn_out, w_pool_in, w_pool_group, w_pool_out = _diff_attn(
        qkvz, rel_bias, attn_lambda[0], attn_subln_gain[0], n_heads, _lambda_init(0), riders)
    x1 = _matmul_residual(og, w_attn_out, xs, "attn_out")

    pz = _pool_proj(x1, norm_gains[1], w_pool_in, n_groups)
    mixed = _pool_group(pz, w_pool_group.reshape(n_groups, group_dim, group_dim), pool_scale[0])
    x2 = _matmul_residual(mixed, w_pool_out, x1, "pool_out")

    return _final_norm(x2, final_norm_gain).reshape(b, s, d)
```
